```python
import jax, jax.numpy as jnp
from jax import lax
import numpy as np

D_MODEL = 2048
BATCH = 2
SEQ = 4096
DEPTH = 4

ATTN_WIDTH = D_MODEL // 2
HEAD_DIM_ATTN = 128
N_HEADS_ATTN = ATTN_WIDTH // HEAD_DIM_ATTN
DILATION_PATTERNS = ((128, 1), (512, 4), (2048, 16))
RET_WIDTH = D_MODEL // 2
N_HEADS_RET = 4
RET_V_DIM = RET_WIDTH // N_HEADS_RET
RET_QK_DIM = RET_V_DIM // 2
RET_CHUNK = 128
MIX_WIDTH = ATTN_WIDTH + RET_WIDTH
IN_SPLIT_SIZES = (ATTN_WIDTH, ATTN_WIDTH, ATTN_WIDTH, ATTN_WIDTH,
                  N_HEADS_RET * RET_QK_DIM, N_HEADS_RET * RET_QK_DIM, RET_WIDTH, RET_WIDTH)
IN_PROJ_WIDTH = sum(IN_SPLIT_SIZES)
IN_SPLIT_IDX = tuple(int(i) for i in np.cumsum(IN_SPLIT_SIZES)[:-1])
NORM_EPS = 1e-6
MASK_VALUE = -1e30

kernel_name = "hybrid_dilated_attn_retention_encoder"


def rms_norm(x, g):
    xf = x.astype(jnp.float32)
    y = xf * lax.rsqrt(jnp.mean(xf * xf, axis=-1, keepdims=True) + NORM_EPS)
    return (y * g.astype(jnp.float32)).astype(x.dtype)


def alibi_slopes(n_heads):
    return jnp.exp2(-8.0 * (jnp.arange(n_heads, dtype=jnp.float32) + 1.0) / n_heads)


def dilated_window_attention(q, k, v, window, dilation, slopes):
    b, s, h, e = q.shape
    radius = window // (2 * dilation)
    blk = radius
    sub_len = s // dilation
    n_blk = -(-sub_len // blk)
    pad_len = n_blk * blk
    def to_sub(t):
        return t.reshape(b, sub_len, dilation, h, e)
    qs = jnp.pad(to_sub(q), ((0, 0), (0, pad_len - sub_len), (0, 0), (0, 0), (0, 0)))
    qb = qs.reshape(b, n_blk, blk, dilation, h, e)
    def key_windows(t):
        tp = jnp.pad(to_sub(t), ((0, 0), (blk, blk + pad_len - sub_len), (0, 0), (0, 0), (0, 0)))
        tb = tp.reshape(b, n_blk + 2, blk, dilation, h, e)
        return jnp.concatenate([tb[:, :-2], tb[:, 1:-1], tb[:, 2:]], axis=2)
    kw = key_windows(k)
    vw = key_windows(v)
    scores = jnp.einsum('bnirhe,bnjrhe->bnrhij', qb, kw).astype(jnp.float32)
    i_idx = jnp.arange(blk)[:, None]
    j_idx = jnp.arange(3 * blk)[None, :]
    rel = j_idx - blk - i_idx
    key_pos = jnp.arange(n_blk)[:, None] * blk + jnp.arange(3 * blk)[None, :] - blk
    valid = (key_pos >= 0) & (key_pos < sub_len)
    mask = (jnp.abs(rel) <= radius)[None] & valid[:, None, :]
    dist = (jnp.abs(rel) * dilation).astype(jnp.float32)
    bias = -slopes[:, None, None] * dist[None]
    scores = jnp.where(mask[None, :, None, None], scores + bias[None, None, None], MASK_VALUE)
    m = jnp.max(scores, axis=-1, keepdims=True)
    p = jnp.exp(scores - m)
    den = jnp.sum(p, axis=-1)
    o = jnp.einsum('bnrhij,bnjrhe->bnirhe', p, vw.astype(jnp.float32))
    o = o / den.transpose(0, 1, 4, 2, 3)[..., None]
    lse = (m[..., 0] + jnp.log(den)).transpose(0, 1, 4, 2, 3)
    o = o.reshape(b, pad_len, dilation, h, e)[:, :sub_len].reshape(b, s, h, e)
    lse = lse.reshape(b, pad_len, dilation, h)[:, :sub_len].reshape(b, s, h)
    return o, lse


def dilated_attention_mixture(q, k, v):
    slopes = alibi_slopes(q.shape[2])
    outs, lses = [], []
    for window, dilation in DILATION_PATTERNS:
        o, lse = dilated_window_attention(q, k, v, window, dilation, slopes)
        outs.append(o)
        lses.append(lse)
    w = jax.nn.softmax(jnp.stack(lses, axis=0), axis=0)
    o = jnp.sum(w[..., None] * jnp.stack(outs, axis=0), axis=0)
    return o.astype(v.dtype)


def retention_one_direction(q, k, v, log_gamma):
    b, s, h, dk = q.shape
    dv = v.shape[-1]
    c = RET_CHUNK
    n = s // c
    def chunk(t):
        return t.reshape(b, n, c, h, t.shape[-1]).transpose(1, 0, 3, 2, 4)
    qc, kc, vc = chunk(q), chunk(k), chunk(v)
    idx = jnp.arange(c, dtype=jnp.float32)
    rel = idx[:, None] - idx[None, :]
    lg = log_gamma[:, None, None]
    decay = jnp.where(rel[None] >= 0, jnp.exp(jnp.maximum(rel, 0.0)[None] * lg), 0.0).astype(q.dtype)
    xi = jnp.exp((idx[None] + 1.0) * log_gamma[:, None]).astype(q.dtype)[..., None]
    zeta = jnp.exp((c - 1.0 - idx[None]) * log_gamma[:, None]).astype(q.dtype)[..., None]
    g_chunk = jnp.exp(c * log_gamma).astype(q.dtype)[:, None, None]

    def step(state, inp):
        qi, ki, vi = inp
        inner = jnp.einsum('bhid,bhjd->bhij', qi, ki) * decay
        o = jnp.einsum('bhij,bhje->bhie', inner, vi) + jnp.einsum('bhid,bhde->bhie', qi, state) * xi
        state = state * g_chunk + jnp.einsum('bhjd,bhje->bhde', ki * zeta, vi)
        return state, o

    state0 = jnp.zeros((b, h, dk, dv), dtype=q.dtype)
    _, oc = lax.scan(step, state0, (qc, kc, vc))
    return oc.transpose(1, 0, 3, 2, 4).reshape(b, s, h, dv)


def bidirectional_retention(q, k, v, decay_logit_f, decay_logit_b):
    lg_f = jax.nn.log_sigmoid(decay_logit_f.astype(jnp.float32))
    lg_b = jax.nn.log_sigmoid(decay_logit_b.astype(jnp.float32))
    o_f = retention_one_direction(q, k, v, lg_f)
    o_b = jnp.flip(retention_one_direction(jnp.flip(q, 1), jnp.flip(k, 1), jnp.flip(v, 1), lg_b), 1)
    o = (o_f + o_b).astype(jnp.float32)
    o = o * lax.rsqrt(jnp.mean(o * o, axis=-1, keepdims=True) + NORM_EPS)
    return o.astype(v.dtype)


def hybrid_layer(x, c_act, g, w_ada, b_ada, w_in, w_out, dec_f, dec_b):
    b, s, _ = x.shape
    mod = c_act @ w_ada + b_ada
    shift, scale, gate = jnp.split(mod, 3, axis=-1)
    h = rms_norm(x, g) * (1.0 + scale[:, None]) + shift[:, None]
    proj = jnp.einsum('bsd,df->bsf', h, w_in)
    q_a, k_a, v_a, z_a, q_r, k_r, v_r, z_r = jnp.split(proj, IN_SPLIT_IDX, axis=-1)
    q_a = q_a.reshape(b, s, N_HEADS_ATTN, HEAD_DIM_ATTN) * (HEAD_DIM_ATTN ** -0.5)
    k_a = k_a.reshape(b, s, N_HEADS_ATTN, HEAD_DIM_ATTN)
    v_a = v_a.reshape(b, s, N_HEADS_ATTN, HEAD_DIM_ATTN)
    y_a = dilated_attention_mixture(q_a, k_a, v_a).reshape(b, s, ATTN_WIDTH)
    q_r = q_r.reshape(b, s, N_HEADS_RET, RET_QK_DIM)
    k_r = k_r.reshape(b, s, N_HEADS_RET, RET_QK_DIM) * (RET_QK_DIM ** -0.5)
    v_r = v_r.reshape(b, s, N_HEADS_RET, RET_V_DIM)
    y_r = bidirectional_retention(q_r, k_r, v_r, dec_f, dec_b).reshape(b, s, RET_WIDTH)
    y = jnp.concatenate([y_a * jax.nn.silu(z_a), y_r * jax.nn.silu(z_r)], axis=-1)
    out = jnp.einsum('bsf,fd->bsd', y, w_out)
    return x + gate[:, None] * out


def setup_inputs(seed: int = 0) -> dict:
    key = jax.random.key(seed)
    ks = jax.random.split(key, 10)
    f32 = jnp.float32
    x = jax.random.normal(ks[0], (BATCH, SEQ, D_MODEL), f32)
    c = jax.random.normal(ks[1], (BATCH, D_MODEL), f32)
    norm_gain = 1.0 + 0.02 * jax.random.normal(ks[2], (DEPTH, D_MODEL), f32)
    w_ada = 0.5 * D_MODEL ** -0.5 * jax.random.normal(ks[3], (DEPTH, D_MODEL, 3 * D_MODEL), f32)
    b_ada = 0.02 * jax.random.normal(ks[4], (DEPTH, 3 * D_MODEL), f32)
    w_in = D_MODEL ** -0.5 * jax.random.normal(ks[5], (DEPTH, D_MODEL, IN_PROJ_WIDTH), f32)
    w_out = MIX_WIDTH ** -0.5 * jax.random.normal(ks[6], (DEPTH, MIX_WIDTH, D_MODEL), f32)
    gamma = 1.0 - jnp.exp2(-5.0 - jnp.arange(N_HEADS_RET, dtype=f32))
    base_logit = jnp.log(gamma) - jnp.log1p(-gamma)
    ret_decay_logit_f = base_logit[None] + 0.1 * jax.random.normal(ks[7], (DEPTH, N_HEADS_RET), f32)
    ret_decay_logit_b = base_logit[None] + 0.1 * jax.random.normal(ks[8], (DEPTH, N_HEADS_RET), f32)
    final_gain = 1.0 + 0.02 * jax.random.normal(ks[9], (D_MODEL,), f32)
    return {"x": x, "c": c, "norm_gain": norm_gain, "w_ada": w_ada, "b_ada": b_ada,
            "w_in": w_in, "w_out": w_out, "ret_decay_logit_f": ret_decay_logit_f,
            "ret_decay_logit_b": ret_decay_logit_b, "final_gain": final_gain}


def reference(x, c, norm_gain, w_ada, b_ada, w_in, w_out, ret_decay_logit_f, ret_decay_logit_b, final_gain):
    c_act = jax.nn.silu(c)
    h = x
    for layer in range(DEPTH):
        h = hybrid_layer(h, c_act, norm_gain[layer], w_ada[layer], b_ada[layer], w_in[layer],
                         w_out[layer], ret_decay_logit_f[layer], ret_decay_logit_b[layer])
    return rms_norm(h, final_gain)
```

```python
import functools

import jax
import jax.numpy as jnp
from jax import lax
from jax.experimental import pallas as pl
from jax.experimental.pallas import tpu as pltpu

F32 = jnp.float32
BF16 = jnp.bfloat16

HEAD_DIM_ATTN = 128
DILATION_PATTERNS = ((128, 1), (512, 4), (2048, 16))
N_HEADS_RET = 4
RET_CHUNK = 128
NORM_EPS = 1e-6

LANE = 128
V7X_VMEM_BYTES = 64 * 2**20
VMEM_CAP_BYTES = 60000 * 1024

ROW_TILE = 512
OUT_ROW_TILE = 256
IN_COL_TILE = 1024
ADA_COL_TILE = 1024
ATTN_TQ = 128
CAST_ROWS = 256
C_PAD = 16


def _vmem_limit(nbytes):
    return int(min(VMEM_CAP_BYTES, nbytes))


def _silu(v):
    return v * jax.nn.sigmoid(v)


def _cast_weight(w_ref, wbf_ref):
    n = w_ref.shape[0] // CAST_ROWS

    def body(i, carry):
        rows = pl.ds(pl.multiple_of(i * CAST_ROWS, CAST_ROWS), CAST_ROWS)
        wbf_ref[rows, :] = w_ref[rows, :].astype(BF16)
        return carry

    lax.fori_loop(0, n, body, 0)


def _ada_kernel(c_ref, w_ref, b_ref, o_ref):
    c_act = _silu(c_ref[...])
    acc = jnp.dot(c_act.astype(BF16), w_ref[...].astype(BF16), preferred_element_type=F32)
    o_ref[...] = acc + b_ref[...]


def _adaln(c_pad, w_ada, b_ada):
    depth, d, n3 = w_ada.shape
    tn = ADA_COL_TILE
    est = 2 * d * tn * 4 + d * tn * 2 + 4 * C_PAD * (d + 2 * tn) * 4
    return pl.pallas_call(
        _ada_kernel,
        grid=(depth, n3 // tn),
        in_specs=[
            pl.BlockSpec((C_PAD, d), lambda l, j: (0, 0)),
            pl.BlockSpec((None, d, tn), lambda l, j: (l, 0, j)),
            pl.BlockSpec((None, 1, tn), lambda l, j: (l, 0, j)),
        ],
        out_specs=pl.BlockSpec((None, C_PAD, tn), lambda l, j: (l, 0, j)),
        out_shape=jax.ShapeDtypeStruct((depth, C_PAD, n3), F32),
        compiler_params=pltpu.CompilerParams(
            dimension_semantics=("arbitrary", "arbitrary"),
            vmem_limit_bytes=_vmem_limit(est + 8 * 2**20)),
        name="adaln_mod",
    )(c_pad, w_ada, b_ada.reshape(depth, 1, n3))


def _modulated_norm(x, g, shift, scale):
    y = x * lax.rsqrt(jnp.mean(x * x, axis=-1, keepdims=True) + NORM_EPS) * g
    return y * (1.0 + scale) + shift


def _modulate_kernel(x_ref, g_ref, mod_ref, h_ref):
    h = _modulated_norm(x_ref[...], g_ref[...], mod_ref[0:1, :], mod_ref[1:2, :])
    h_ref[...] = h.astype(BF16)


def _modulate(x, gain, mod, layer):
    b, s, d = x.shape
    tm = ROW_TILE
    est = 2 * tm * d * (4 + 2) + 3 * tm * d * 4
    return pl.pallas_call(
        _modulate_kernel,
        grid=(b, s // tm),
        in_specs=[
            pl.BlockSpec((None, tm, d), lambda bi, m: (bi, m, 0)),
            pl.BlockSpec((None, 1, d), lambda bi, m: (layer, 0, 0)),
            pl.BlockSpec((None, None, 3, d), lambda bi, m: (layer, bi, 0, 0)),
        ],
        out_specs=pl.BlockSpec((None, tm, d), lambda bi, m: (bi, m, 0)),
        out_shape=jax.ShapeDtypeStruct((b, s, d), BF16),
        compiler_params=pltpu.CompilerParams(
            dimension_semantics=("arbitrary", "arbitrary"),
            vmem_limit_bytes=_vmem_limit(est + 8 * 2**20)),
        name="modulate0",
    )(x, gain.reshape(gain.shape[0], 1, d), mod)


def _inproj_kernel(h_ref, w_ref, cs_ref, o_ref, wbf_ref):
    @pl.when((pl.program_id(1) == 0) & (pl.program_id(2) == 0))
    def _():
        _cast_weight(w_ref, wbf_ref)

    acc = jnp.dot(h_ref[...], wbf_ref[...], preferred_element_type=F32)
    acc = acc * cs_ref[...]
    for j in range(o_ref.shape[0]):
        o_ref[j] = acc[:, j * LANE:(j + 1) * LANE].astype(o_ref.dtype)


def _inproj(h, w_in, colscale, layer, n_tiles, tile_of, out_dtype, name):
    b, s, d = h.shape
    tm, tn = ROW_TILE, IN_COL_TILE
    nb = tn // LANE
    osz = jnp.dtype(out_dtype).itemsize
    est = 2 * d * tn * 4 + d * tn * 2 + 2 * tm * d * 2 + 2 * tm * tn * osz + 2 * tm * tn * 4
    return pl.pallas_call(
        _inproj_kernel,
        grid=(n_tiles, b, s // tm),
        in_specs=[
            pl.BlockSpec((None, tm, d), lambda j, bi, m: (bi, m, 0)),
            pl.BlockSpec((None, d, tn), lambda j, bi, m: (layer, 0, tile_of(j))),
            pl.BlockSpec((1, tn), lambda j, bi, m: (0, tile_of(j))),
        ],
        out_specs=pl.BlockSpec((None, nb, tm, LANE), lambda j, bi, m: (bi, j, m, 0)),
        out_shape=jax.ShapeDtypeStruct((b, n_tiles * nb, s, LANE), out_dtype),
        scratch_shapes=[pltpu.VMEM((d, tn), BF16)],
        compiler_params=pltpu.CompilerParams(
            dimension_semantics=("arbitrary", "arbitrary", "arbitrary"),
            vmem_limit_bytes=_vmem_limit(est + 8 * 2**20)),
        name=name,
    )(h, w_in, colscale)


def _attn_kernel(*refs, n_heads, patterns, seq):
    npat = len(patterns)
    qkv_refs = refs[:3 * npat]
    z_ref, o_ref, acc_sc, m_sc, l_sc, bias_sc = refs[3 * npat:]
    tq = ATTN_TQ
    radius = patterns[0][0] // (2 * patterns[0][1])
    tk = tq + 2 * radius

    head = pl.program_id(1)
    expo = (head + 1).astype(F32) * (-8.0 / n_heads)
    slope = jnp.exp2(jnp.full((tq, tk), 1.0, F32) * expo)
    row = lax.broadcasted_iota(jnp.int32, (tq, tk), 0)
    col = lax.broadcasted_iota(jnp.int32, (tq, tk), 1)
    for p, (_, dil) in enumerate(patterns):
        for case in range(3):
            rel = jnp.abs(col - row - case * radius)
            bias = -slope * (rel * dil).astype(F32)
            bias_sc[p * 3 + case] = jnp.where(rel <= radius, bias, -jnp.inf)

    for p, (_, dil) in enumerate(patterns):
        sub_len = seq // dil
        nblk = sub_len // tq
        q_ref, k_ref, v_ref = qkv_refs[3 * p:3 * p + 3]
        for r in range(dil):
            lanes = slice(r * LANE, (r + 1) * LANE)

            def block(i, carry, p=p, dil=dil, r=r, lanes=lanes, sub_len=sub_len,
                      q_ref=q_ref, k_ref=k_ref, v_ref=v_ref):
                l0 = pl.multiple_of(i * tq, tq)
                ks = pl.multiple_of(jnp.clip(l0 - radius, 0, sub_len - tk), radius)
                case = (l0 - ks) // radius
                q = q_ref[pl.ds(l0, tq), lanes]
                k = k_ref[pl.ds(ks, tk), lanes]
                v = v_ref[pl.ds(ks, tk), lanes]
                s = lax.dot_general(q, k, (((1,), (1,)), ((), ())),
                                    preferred_element_type=F32)
                s = s + bias_sc[p * 3 + case]
                m = jnp.max(s, axis=-1, keepdims=True)
                pe = jnp.exp(s - m)
                den = jnp.sum(pe, axis=-1, keepdims=True)
                acc = jnp.dot(pe.astype(BF16), v, preferred_element_type=F32)
                if dil == 1:
                    rows = pl.ds(l0, tq)
                else:
                    rows = pl.ds(l0 * dil + r, tq, stride=dil)
                acc_sc[p, rows, :] = acc
                m_sc[p, rows, :] = jnp.broadcast_to(m, (tq, LANE))
                l_sc[p, rows, :] = jnp.broadcast_to(den, (tq, LANE))
                return carry

            lax.fori_loop(0, nblk, block, 0, unroll=min(nblk, 4))

    ch = 256

    def merge(i, carry):
        rows = pl.ds(pl.multiple_of(i * ch, ch), ch)
        ms = [m_sc[p, rows, :] for p in range(npat)]
        mx = functools.reduce(jnp.maximum, ms)
        num = jnp.zeros((ch, LANE), F32)
        den = jnp.zeros((ch, LANE), F32)
        for p in range(npat):
            e = jnp.exp(ms[p] - mx)
            num = num + e * acc_sc[p, rows, :]
            den = den + e * l_sc[p, rows, :]
        o_ref[rows, :] = ((num / den) * _silu(z_ref[rows, :])).astype(o_ref.dtype)
        return carry

    lax.fori_loop(0, seq // ch, merge, 0)


def _attention(qkv, z, n_heads):
    b, _, s, _ = qkv.shape
    patterns = DILATION_PATTERNS
    radius = patterns[0][0] // (2 * patterns[0][1])
    assert all(w // (2 * dl) == radius for w, dl in patterns)
    assert all((s // dl) % ATTN_TQ == 0 and s // dl >= ATTN_TQ + 2 * radius for _, dl in patterns)
    tk = ATTN_TQ + 2 * radius
    npat = len(patterns)

    operands, in_specs = [], []
    for _, dl in patterns:
        view = qkv.reshape(b, qkv.shape[1], s // dl, dl * LANE)
        for t in range(3):
            operands.append(view)
            in_specs.append(pl.BlockSpec(
                (None, None, s // dl, dl * LANE),
                lambda bi, h, t=t: (bi, t * n_heads + h, 0, 0)))
    operands.append(z)
    in_specs.append(pl.BlockSpec((None, None, s, LANE), lambda bi, h: (bi, h, 0, 0)))

    est = (2 * 3 * npat * s * LANE * 2 + 2 * s * LANE * 4 + 2 * s * LANE * 2
           + 3 * npat * s * LANE * 4 + 3 * npat * ATTN_TQ * tk * 4)
    return pl.pallas_call(
        functools.partial(_attn_kernel, n_heads=n_heads, patterns=patterns, seq=s),
        grid=(b, n_heads),
        in_specs=in_specs,
        out_specs=pl.BlockSpec((None, None, s, LANE), lambda bi, h: (bi, h, 0, 0)),
        out_shape=jax.ShapeDtypeStruct((b, n_heads, s, LANE), BF16),
        scratch_shapes=[
            pltpu.VMEM((npat, s, LANE), F32),
            pltpu.VMEM((npat, s, LANE), F32),
            pltpu.VMEM((npat, s, LANE), F32),
            pltpu.VMEM((3 * npat, ATTN_TQ, tk), F32),
        ],
        compiler_params=pltpu.CompilerParams(
            dimension_semantics=("arbitrary", "arbitrary"),
            vmem_limit_bytes=_vmem_limit(est + 8 * 2**20)),
        name="dilated_attention",
    )(*operands)


def _log_sigmoid(v):
    return jnp.minimum(v, 0.0) - jnp.log1p(jnp.exp(-jnp.abs(v)))


def _ret_kernel(q_ref, k_ref, v_ref, z_ref, dl_ref, o_ref, of_sc, ob_sc, sf_sc, sb_sc, *, seq):
    c = RET_CHUNK
    n = seq // c
    dk = q_ref.shape[-1]
    dv = 2 * LANE

    lg_f = _log_sigmoid(dl_ref[0, 0:1, :])
    lg_b = _log_sigmoid(dl_ref[1, 0:1, :])
    ri = lax.broadcasted_iota(jnp.int32, (c, c), 0)
    ci = lax.broadcasted_iota(jnp.int32, (c, c), 1)
    rel = (ri - ci).astype(F32)
    dec_f = jnp.where(rel >= 0, jnp.exp(jnp.maximum(rel, 0.0) * lg_f[:, :c]), 0.0)
    dec_b = jnp.where(rel <= 0, jnp.exp(jnp.maximum(-rel, 0.0) * lg_b[:, :c]), 0.0)
    idx_v = lax.broadcasted_iota(jnp.int32, (c, dv), 0).astype(F32)
    idx_k = lax.broadcasted_iota(jnp.int32, (c, dk), 0).astype(F32)
    xi_f = jnp.exp((idx_v + 1.0) * lg_f)
    xi_b = jnp.exp((c - idx_v) * lg_b)
    zeta_f = jnp.exp((c - 1.0 - idx_k) * lg_f[:, :dk])
    zeta_b = jnp.exp(idx_k * lg_b[:, :dk])
    g_f = jnp.exp(c * lg_f)
    g_b = jnp.exp(c * lg_b)

    sf_sc[...] = jnp.zeros_like(sf_sc)
    sb_sc[...] = jnp.zeros_like(sb_sc)

    def one_chunk(rows, dec, xi, zeta, g, s_sc, o_sc):
        q = q_ref[rows, :]
        k = k_ref[rows, :]
        v = jnp.concatenate([v_ref[0, rows, :], v_ref[1, rows, :]], axis=-1)
        inner = lax.dot_general(q, k, (((1,), (1,)), ((), ())), preferred_element_type=F32) * dec
        state = s_sc[...]
        o = jnp.dot(inner.astype(BF16), v, preferred_element_type=F32)
        o = o + jnp.dot(q, state.astype(BF16), preferred_element_type=F32) * xi
        o_sc[rows, :] = o
        kz_t = (k.astype(F32) * zeta).T.astype(BF16)
        s_sc[...] = state * g + jnp.dot(kz_t, v, preferred_element_type=F32)

    def step(i, carry):
        rows_f = pl.ds(pl.multiple_of(i * c, c), c)
        rows_b = pl.ds(pl.multiple_of((n - 1 - i) * c, c), c)
        one_chunk(rows_f, dec_f, xi_f, zeta_f, g_f, sf_sc, of_sc)
        one_chunk(rows_b, dec_b, xi_b, zeta_b, g_b, sb_sc, ob_sc)
        return carry

    lax.fori_loop(0, n, step, 0)

    def finish(i, carry):
        rows = pl.ds(pl.multiple_of(i * c, c), c)
        o = of_sc[rows, :] + ob_sc[rows, :]
        y = o * lax.rsqrt(jnp.mean(o * o, axis=-1, keepdims=True) + NORM_EPS)
        z = jnp.concatenate([z_ref[0, rows, :], z_ref[1, rows, :]], axis=-1)
        out = (y * _silu(z)).astype(o_ref.dtype)
        o_ref[0, rows, :] = out[:, :LANE]
        o_ref[1, rows, :] = out[:, LANE:]
        return carry

    lax.fori_loop(0, n, finish, 0)


def _retention(qkv, z, dl, q_blk, k_blk, v_blk2, z_blk2):
    b, _, s, _ = qkv.shape
    nh = N_HEADS_RET
    dv = 2 * LANE
    est = (2 * 2 * s * LANE * 2 + 2 * s * dv * 2 + 2 * s * dv * 4 + 2 * s * dv * 2
           + 2 * s * dv * 4 + 2 * LANE * dv * 4)
    return pl.pallas_call(
        functools.partial(_ret_kernel, seq=s),
        grid=(b, nh),
        in_specs=[
            pl.BlockSpec((None, None, s, LANE), lambda bi, h: (bi, q_blk + h, 0, 0)),
            pl.BlockSpec((None, None, s, LANE), lambda bi, h: (bi, k_blk + h, 0, 0)),
            pl.BlockSpec((None, 2, s, LANE), lambda bi, h: (bi, v_blk2 + h, 0, 0)),
            pl.BlockSpec((None, 2, s, LANE), lambda bi, h: (bi, z_blk2 + h, 0, 0)),
            pl.BlockSpec((None, 2, 8, dv), lambda bi, h: (h, 0, 0, 0)),
        ],
        out_specs=pl.BlockSpec((None, 2, s, LANE), lambda bi, h: (bi, h, 0, 0)),
        out_shape=jax.ShapeDtypeStruct((b, 2 * nh, s, LANE), BF16),
        scratch_shapes=[
            pltpu.VMEM((s, dv), F32),
            pltpu.VMEM((s, dv), F32),
            pltpu.VMEM((LANE, dv), F32),
            pltpu.VMEM((LANE, dv), F32),
        ],
        compiler_params=pltpu.CompilerParams(
            dimension_semantics=("arbitrary", "arbitrary"),
            vmem_limit_bytes=_vmem_limit(est + 8 * 2**20)),
        name="retention",
    )(qkv, qkv, qkv, z, dl)


def _outproj_kernel(ya_ref, yr_ref, w_ref, x_ref, mod_ref, gn_ref, modn_ref, *rest, last):
    if last:
        o_ref, wbf_ref = rest
    else:
        xo_ref, h_ref, wbf_ref = rest

    @pl.when((pl.program_id(0) == 0) & (pl.program_id(1) == 0))
    def _():
        _cast_weight(w_ref, wbf_ref)

    y = jnp.concatenate([ya_ref[j] for j in range(ya_ref.shape[0])]
                        + [yr_ref[j] for j in range(yr_ref.shape[0])], axis=-1)
    out = jnp.dot(y, wbf_ref[...], preferred_element_type=F32)
    xn = x_ref[...] + mod_ref[2:3, :] * out
    if last:
        y = xn * lax.rsqrt(jnp.mean(xn * xn, axis=-1, keepdims=True) + NORM_EPS) * gn_ref[...]
        o_ref[...] = y
    else:
        xo_ref[...] = xn
        h = _modulated_norm(xn, gn_ref[...], modn_ref[0:1, :], modn_ref[1:2, :])
        h_ref[...] = h.astype(BF16)


def _outproj(ya, yr, w_out, x, mod, gains, layer, last):
    b, s, d = x.shape
    tm = OUT_ROW_TILE
    nba, nbr = ya.shape[1], yr.shape[1]
    kdim = (nba + nbr) * LANE
    nxt = layer if last else layer + 1
    est = (kdim * d * 4 + kdim * d * 2 + 2 * tm * kdim * 2 + 2 * tm * d * 4
           + 2 * tm * d * 4 + 2 * tm * d * 2 + 3 * tm * d * 4)
    if last:
        out_shape = jax.ShapeDtypeStruct((b, s, d), F32)
        out_specs = pl.BlockSpec((None, tm, d), lambda bi, m: (bi, m, 0))
    else:
        out_shape = (jax.ShapeDtypeStruct((b, s, d), F32), jax.ShapeDtypeStruct((b, s, d), BF16))
        out_specs = (pl.BlockSpec((None, tm, d), lambda bi, m: (bi, m, 0)),
                     pl.BlockSpec((None, tm, d), lambda bi, m: (bi, m, 0)))
    return pl.pallas_call(
        functools.partial(_outproj_kernel, last=last),
        grid=(b, s // tm),
        in_specs=[
            pl.BlockSpec((None, nba, tm, LANE), lambda bi, m: (bi, 0, m, 0)),
            pl.BlockSpec((None, nbr, tm, LANE), lambda bi, m: (bi, 0, m, 0)),
            pl.BlockSpec((None, kdim, d), lambda bi, m: (layer, 0, 0),
                         pipeline_mode=pl.Buffered(1)),
            pl.BlockSpec((None, tm, d), lambda bi, m: (bi, m, 0)),
            pl.BlockSpec((None, None, 3, d), lambda bi, m: (layer, bi, 0, 0)),
            pl.BlockSpec((None, 1, d), lambda bi, m: (layer + 1, 0, 0)),
            pl.BlockSpec((None, None, 3, d), lambda bi, m: (nxt, bi, 0, 0)),
        ],
        out_specs=out_specs,
        out_shape=out_shape,
        scratch_shapes=[pltpu.VMEM((kdim, d), BF16)],
        compiler_params=pltpu.CompilerParams(
            dimension_semantics=("arbitrary", "arbitrary"),
            vmem_limit_bytes=_vmem_limit(est + 4 * 2**20)),
        name="outproj_last" if last else "outproj",
    )(ya, yr, w_out, x, mod, gains, mod)


def kernel(x, c, norm_gain, w_ada, b_ada, w_in, w_out, ret_decay_logit_f, ret_decay_logit_b,
           final_gain):
    b, s, d = x.shape
    depth = w_in.shape[0]
    attn_w = d // 2
    ret_w = d // 2
    n_heads = attn_w // HEAD_DIM_ATTN
    ret_v = ret_w // N_HEADS_RET
    ret_qk = ret_v // 2
    assert ret_qk == LANE and ret_v == 2 * LANE and HEAD_DIM_ATTN == LANE
    assert attn_w == IN_COL_TILE and N_HEADS_RET * ret_qk * 2 == IN_COL_TILE
    assert w_in.shape[2] == 7 * IN_COL_TILE and s % ROW_TILE == 0

    colscale = jnp.ones((7, IN_COL_TILE), F32)
    colscale = colscale.at[0, :].set(HEAD_DIM_ATTN ** -0.5)
    colscale = colscale.at[4, N_HEADS_RET * ret_qk:].set(ret_qk ** -0.5)
    colscale = colscale.reshape(1, 7 * IN_COL_TILE)

    c_pad = jnp.zeros((C_PAD, d), F32).at[:b].set(c)
    mod = _adaln(c_pad, w_ada, b_ada).reshape(depth, C_PAD, 3, d)
    gains = jnp.concatenate([norm_gain, final_gain[None]], axis=0).reshape(depth + 1, 1, d)

    def lanes(v):
        return jnp.broadcast_to(v[:, :, None, None, None], (depth, N_HEADS_RET, 1, 8, 2 * LANE))
    dl = jnp.concatenate([lanes(ret_decay_logit_f), lanes(ret_decay_logit_b)], axis=2)

    h = _modulate(x, gains, mod, 0)
    out = None
    for layer in range(depth):
        qkv = _inproj(h, w_in, colscale, layer, 5, lambda j: j + jnp.where(j >= 3, 1, 0),
                      BF16, "inproj_qkv")
        z = _inproj(h, w_in, colscale, layer, 2, lambda j: 3 + 3 * j, F32, "inproj_gate")
        ya = _attention(qkv, z, n_heads)
        yr = _retention(qkv, z, dl[layer], 3 * n_heads, 3 * n_heads + N_HEADS_RET,
                        (3 * n_heads + 2 * N_HEADS_RET) // 2, n_heads // 2)
        last = layer == depth - 1
        res = _outproj(ya, yr, w_out, x, mod, gains, layer, last)
        if last:
            out = res
        else:
            x, h = res
    return out
```

```python
import functools

import jax
import jax.numpy as jnp
from jax import lax
from jax.experimental import pallas as pl
from jax.experimental.pallas import tpu as pltpu

F32 = jnp.float32
BF16 = jnp.bfloat16

HEAD_DIM_ATTN = 128
DILATION_PATTERNS = ((128, 1), (512, 4), (2048, 16))
N_HEADS_RET = 4
RET_CHUNK = 128
NORM_EPS = 1e-6

LANE = 128
VMEM_CAP_BYTES = 60000 * 1024
TEMP_ALLOWANCE = 8 * 2**20

ROW_TILE = 512
OUT_ROW_TILE = 256
IN_COL_TILE = 1024
ADA_COL_TILE = 1024
ATTN_TQ = 128
ATTN_UNROLL = 8
RET_UNROLL = 4
IN_SUB_ROWS = 256
CAST_ROWS = 256
C_PAD = 16


def _vmem_limit(nbytes):
    return int(min(VMEM_CAP_BYTES, nbytes + TEMP_ALLOWANCE))


def _silu(v):
    return v * jax.nn.sigmoid(v)


def _cast_weight(w_ref, wbf_ref):
    n = w_ref.shape[0] // CAST_ROWS

    def body(i, carry):
        rows = pl.ds(pl.multiple_of(i * CAST_ROWS, CAST_ROWS), CAST_ROWS)
        wbf_ref[rows, :] = w_ref[rows, :].astype(BF16)
        return carry

    lax.fori_loop(0, n, body, 0)


def _ada_kernel(c_ref, w_ref, b_ref, o_ref):
    c_act = _silu(c_ref[...])
    acc = jnp.dot(c_act.astype(BF16), w_ref[...].astype(BF16), preferred_element_type=F32)
    o_ref[...] = acc + b_ref[...]


def _adaln(c_pad, w_ada, b_ada):
    depth, d, n3 = w_ada.shape
    tn = ADA_COL_TILE
    est = 2 * d * tn * 4 + d * tn * 2 + 4 * C_PAD * (d + 2 * tn) * 4
    return pl.pallas_call(
        _ada_kernel,
        grid=(depth, n3 // tn),
        in_specs=[
            pl.BlockSpec((C_PAD, d), lambda l, j: (0, 0)),
            pl.BlockSpec((None, d, tn), lambda l, j: (l, 0, j)),
            pl.BlockSpec((None, 1, tn), lambda l, j: (l, 0, j)),
        ],
        out_specs=pl.BlockSpec((None, C_PAD, tn), lambda l, j: (l, 0, j)),
        out_shape=jax.ShapeDtypeStruct((depth, C_PAD, n3), F32),
        compiler_params=pltpu.CompilerParams(
            dimension_semantics=("arbitrary", "arbitrary"),
            vmem_limit_bytes=_vmem_limit(est)),
        name="adaln_mod",
    )(c_pad, w_ada, b_ada.reshape(depth, 1, n3))


def _modulated_norm(x, g, shift, scale):
    y = x * lax.rsqrt(jnp.mean(x * x, axis=-1, keepdims=True) + NORM_EPS) * g
    return y * (1.0 + scale) + shift


def _modulate_kernel(x_ref, g_ref, mod_ref, h_ref):
    h = _modulated_norm(x_ref[...], g_ref[...], mod_ref[0:1, :], mod_ref[1:2, :])
    h_ref[...] = h.astype(BF16)


def _modulate(x, gains, mod, layer):
    b, s, d = x.shape
    tm = ROW_TILE
    est = 2 * tm * d * (4 + 2) + 3 * tm * d * 4
    return pl.pallas_call(
        _modulate_kernel,
        grid=(b, s // tm),
        in_specs=[
            pl.BlockSpec((None, tm, d), lambda bi, m: (bi, m, 0)),
            pl.BlockSpec((None, 1, d), lambda bi, m: (layer, 0, 0)),
            pl.BlockSpec((None, None, 3, d), lambda bi, m: (layer, bi, 0, 0)),
        ],
        out_specs=pl.BlockSpec((None, tm, d), lambda bi, m: (bi, m, 0)),
        out_shape=jax.ShapeDtypeStruct((b, s, d), BF16),
        compiler_params=pltpu.CompilerParams(
            dimension_semantics=("arbitrary", "arbitrary"),
            vmem_limit_bytes=_vmem_limit(est)),
        name="modulate0",
    )(x, gains, mod)


def _project_tile(h_ref, w_ref, cs_ref, wbf_ref):
    @pl.when((pl.program_id(1) == 0) & (pl.program_id(2) == 0))
    def _():
        _cast_weight(w_ref, wbf_ref)

    acc = jnp.dot(h_ref[...], wbf_ref[...], preferred_element_type=F32)
    return acc * cs_ref[...]


def _inproj_kernel(h_ref, w_ref, cs_ref, o_ref, wbf_ref):
    acc = _project_tile(h_ref, w_ref, cs_ref, wbf_ref)
    for j in range(o_ref.shape[0]):
        o_ref[j] = acc[:, j * LANE:(j + 1) * LANE].astype(o_ref.dtype)


def _inproj_attn_kernel(h_ref, w_ref, cs_ref, *rest, dilations):
    out_refs = rest[:len(dilations)]
    wbf_ref = rest[len(dilations)]
    stage_refs = rest[len(dilations) + 1:]

    @pl.when((pl.program_id(1) == 0) & (pl.program_id(2) == 0))
    def _():
        _cast_weight(w_ref, wbf_ref)

    nb, sub = stage_refs[0].shape[0], stage_refs[0].shape[1]
    for t, stage_ref in enumerate(stage_refs):
        s0 = t * sub
        acc = jnp.dot(h_ref[s0:s0 + sub, :], wbf_ref[...], preferred_element_type=F32)
        acc = acc * cs_ref[...]
        for j in range(nb):
            stage_ref[j] = acc[:, j * LANE:(j + 1) * LANE]
        for dil, o_ref in zip(dilations, out_refs):
            n = sub // dil
            for j in range(nb):
                for r in range(dil):
                    rows = pl.ds(r, n, stride=dil) if dil > 1 else pl.ds(0, n)
                    o_ref[j, r, s0 // dil:s0 // dil + n, :] = (
                        stage_ref[j, rows, :].astype(o_ref.dtype))


def _inproj_specs(h, layer, tile_of):
    b, s, d = h.shape
    tm, tn = ROW_TILE, IN_COL_TILE
    return [
        pl.BlockSpec((None, tm, d), lambda j, bi, m: (bi, m, 0)),
        pl.BlockSpec((None, d, tn), lambda j, bi, m: (layer, 0, tile_of(j))),
        pl.BlockSpec((1, tn), lambda j, bi, m: (0, tile_of(j))),
    ]


def _inproj(h, w_in, colscale, layer, n_tiles, tile_of, out_dtype, name):
    b, s, d = h.shape
    tm, tn = ROW_TILE, IN_COL_TILE
    nb = tn // LANE
    osz = jnp.dtype(out_dtype).itemsize
    est = 2 * d * tn * 4 + d * tn * 2 + 2 * tm * d * 2 + 2 * tm * tn * osz + 2 * tm * tn * 4
    return pl.pallas_call(
        _inproj_kernel,
        grid=(n_tiles, b, s // tm),
        in_specs=_inproj_specs(h, layer, tile_of),
        out_specs=pl.BlockSpec((None, nb, tm, LANE), lambda j, bi, m: (bi, j, m, 0)),
        out_shape=jax.ShapeDtypeStruct((b, n_tiles * nb, s, LANE), out_dtype),
        scratch_shapes=[pltpu.VMEM((d, tn), BF16)],
        compiler_params=pltpu.CompilerParams(
            dimension_semantics=("arbitrary", "arbitrary", "arbitrary"),
            vmem_limit_bytes=_vmem_limit(est)),
        name=name,
    )(h, w_in, colscale)


def _inproj_attn(h, w_in, colscale, layer, n_tiles, tile_of, dilations):
    b, s, d = h.shape
    tm, tn = ROW_TILE, IN_COL_TILE
    nb = tn // LANE
    nd = len(dilations)
    est = (2 * d * tn * 4 + d * tn * 2 + 2 * tm * d * 2 + nd * 2 * tm * tn * 2
           + 3 * tm * tn * 4)
    return pl.pallas_call(
        functools.partial(_inproj_attn_kernel, dilations=dilations),
        grid=(n_tiles, b, s // tm),
        in_specs=_inproj_specs(h, layer, tile_of),
        out_specs=[pl.BlockSpec((None, nb, dl, tm // dl, LANE),
                                lambda j, bi, m: (bi, j, 0, m, 0)) for dl in dilations],
        out_shape=[jax.ShapeDtypeStruct((b, n_tiles * nb, dl, s // dl, LANE), BF16)
                   for dl in dilations],
        scratch_shapes=[pltpu.VMEM((d, tn), BF16)]
        + [pltpu.VMEM((nb, IN_SUB_ROWS, LANE), F32)] * (tm // IN_SUB_ROWS),
        compiler_params=pltpu.CompilerParams(
            dimension_semantics=("arbitrary", "arbitrary", "arbitrary"),
            vmem_limit_bytes=_vmem_limit(est)),
        name="inproj_attn",
    )(h, w_in, colscale)


def _attn_kernel(*refs, n_heads, patterns, seq):
    npat = len(patterns)
    qkv_refs = refs[:3 * npat]
    z_ref, o_ref, acc_sc, lse_sc, bias_sc = refs[3 * npat:]
    tq = ATTN_TQ
    radius = patterns[0][0] // (2 * patterns[0][1])
    tk = tq + 2 * radius

    head = pl.program_id(1)
    expo = (head + 1).astype(F32) * (-8.0 / n_heads)
    slope = jnp.exp2(jnp.full((tq, tk), 1.0, F32) * expo)
    row = lax.broadcasted_iota(jnp.int32, (tq, tk), 0)
    col = lax.broadcasted_iota(jnp.int32, (tq, tk), 1)
    for p, (_, dil) in enumerate(patterns):
        for case in range(3):
            rel = jnp.abs(col - row - case * radius)
            bias = -slope * (rel * dil).astype(F32)
            bias_sc[p * 3 + case] = jnp.where(rel <= radius, bias, -jnp.inf)

    nblocks = seq // tq
    for p, (_, dil) in enumerate(patterns):
        sub_len = seq // dil
        nblk = sub_len // tq
        shift = nblk.bit_length() - 1
        q_ref, k_ref, v_ref = qkv_refs[3 * p:3 * p + 3]

        def block(g, carry, p=p, dil=dil, sub_len=sub_len, nblk=nblk, shift=shift,
                  q_ref=q_ref, k_ref=k_ref, v_ref=v_ref):
            r = lax.shift_right_logical(g, shift)
            i = lax.bitwise_and(g, nblk - 1)
            l0 = pl.multiple_of(i * tq, tq)
            ks = pl.multiple_of(jnp.clip(l0 - radius, 0, sub_len - tk), radius)
            case = (l0 - ks) // radius
            q = q_ref[r, pl.ds(l0, tq), :]
            k = k_ref[r, pl.ds(ks, tk), :]
            v = v_ref[r, pl.ds(ks, tk), :]
            s = lax.dot_general(q, k, (((1,), (1,)), ((), ())), preferred_element_type=F32)
            s = s + bias_sc[p * 3 + case]
            m = jnp.max(s, axis=-1, keepdims=True)
            pe = jnp.exp(s - m)
            den = jnp.sum(pe, axis=-1, keepdims=True)
            acc = jnp.dot(pe.astype(BF16), v, preferred_element_type=F32) / den
            lse = m + jnp.log(den)
            if dil == 1:
                rows = pl.ds(l0, tq)
            else:
                rows = pl.ds(l0 * dil + r, tq, stride=dil)
            acc_sc[p, rows, :] = acc
            lse_sc[p, rows, :] = jnp.broadcast_to(lse, (tq, LANE))
            return carry

        lax.fori_loop(0, nblocks, block, 0, unroll=ATTN_UNROLL)

    ch = 256

    def merge(i, carry):
        rows = pl.ds(pl.multiple_of(i * ch, ch), ch)
        ls = [lse_sc[p, rows, :] for p in range(npat)]
        mx = functools.reduce(jnp.maximum, ls)
        num = jnp.zeros((ch, LANE), F32)
        den = jnp.zeros((ch, LANE), F32)
        for p in range(npat):
            e = jnp.exp(ls[p] - mx)
            num = num + e * acc_sc[p, rows, :]
            den = den + e
        o_ref[rows, :] = ((num / den) * _silu(z_ref[rows, :])).astype(o_ref.dtype)
        return carry

    lax.fori_loop(0, seq // ch, merge, 0)


def _attention(qkv_views, z, n_heads):
    patterns = DILATION_PATTERNS
    b, _, _, s, _ = qkv_views[0].shape
    radius = patterns[0][0] // (2 * patterns[0][1])
    assert all(w // (2 * dl) == radius for w, dl in patterns)
    tk = ATTN_TQ + 2 * radius
    for _, dl in patterns:
        nblk = (s // dl) // ATTN_TQ
        assert (s // dl) % ATTN_TQ == 0 and s // dl >= tk and nblk & (nblk - 1) == 0
    assert (s // ATTN_TQ) % ATTN_UNROLL == 0
    npat = len(patterns)

    operands, in_specs = [], []
    for (_, dl), view in zip(patterns, qkv_views):
        for t in range(3):
            operands.append(view)
            in_specs.append(pl.BlockSpec(
                (None, None, dl, s // dl, LANE),
                lambda bi, h, t=t: (bi, t * n_heads + h, 0, 0, 0)))
    operands.append(z)
    in_specs.append(pl.BlockSpec((None, None, s, LANE), lambda bi, h: (bi, h, 0, 0)))

    est = (2 * 3 * npat * s * LANE * 2 + 2 * s * LANE * 4 + 2 * s * LANE * 2
           + 2 * npat * s * LANE * 4 + 3 * npat * ATTN_TQ * tk * 4)
    return pl.pallas_call(
        functools.partial(_attn_kernel, n_heads=n_heads, patterns=patterns, seq=s),
        grid=(b, n_heads),
        in_specs=in_specs,
        out_specs=pl.BlockSpec((None, None, s, LANE), lambda bi, h: (bi, h, 0, 0)),
        out_shape=jax.ShapeDtypeStruct((b, n_heads, s, LANE), BF16),
        scratch_shapes=[
            pltpu.VMEM((npat, s, LANE), F32),
            pltpu.VMEM((npat, s, LANE), F32),
            pltpu.VMEM((3 * npat, ATTN_TQ, tk), F32),
        ],
        compiler_params=pltpu.CompilerParams(
            dimension_semantics=("arbitrary", "arbitrary"),
            vmem_limit_bytes=_vmem_limit(est)),
        name="dilated_attention",
    )(*operands)


def _log_sigmoid(v):
    return jnp.minimum(v, 0.0) - jnp.log1p(jnp.exp(-jnp.abs(v)))


def _ret_kernel(q_ref, k_ref, v_ref, z_ref, dl_ref, o_ref, oi_sc, uf_sc, ub_sc, st_sc, *, seq):
    c = RET_CHUNK
    n = seq // c
    dk = q_ref.shape[-1]
    dv = 2 * LANE

    lg_f = _log_sigmoid(dl_ref[0, 0:1, :])
    lg_b = _log_sigmoid(dl_ref[1, 0:1, :])
    ri = lax.broadcasted_iota(jnp.int32, (c, c), 0)
    ci = lax.broadcasted_iota(jnp.int32, (c, c), 1)
    rel = (ri - ci).astype(F32)
    dec = (jnp.where(rel >= 0, jnp.exp(jnp.maximum(rel, 0.0) * lg_f[:, :c]), 0.0)
           + jnp.where(rel <= 0, jnp.exp(jnp.maximum(-rel, 0.0) * lg_b[:, :c]), 0.0))
    idx_v = lax.broadcasted_iota(jnp.int32, (c, dv), 0).astype(F32)
    idx_k = lax.broadcasted_iota(jnp.int32, (c, dk), 0).astype(F32)
    xi_f = jnp.exp((idx_v + 1.0) * lg_f)
    xi_b = jnp.exp((c - idx_v) * lg_b)
    zeta_f = jnp.exp((c - 1.0 - idx_k) * lg_f[:, :dk])
    zeta_b = jnp.exp(idx_k * lg_b[:, :dk])
    g_f = jnp.exp(c * lg_f)
    g_b = jnp.exp(c * lg_b)

    def intra(i, carry):
        rows = pl.ds(pl.multiple_of(i * c, c), c)
        q = q_ref[rows, :]
        k = k_ref[rows, :]
        v = jnp.concatenate([v_ref[0, rows, :], v_ref[1, rows, :]], axis=-1)
        qk = lax.dot_general(q, k, (((1,), (1,)), ((), ())), preferred_element_type=F32)
        kf = k.astype(F32)
        lhs = jnp.concatenate([(qk * dec).astype(BF16),
                               (kf * zeta_f).T.astype(BF16),
                               (kf * zeta_b).T.astype(BF16)], axis=0)
        res = jnp.dot(lhs, v, preferred_element_type=F32)
        oi_sc[rows, :] = res[:c]
        uf_sc[i] = res[c:c + dk]
        ub_sc[i] = res[c + dk:]
        return carry

    lax.fori_loop(0, n, intra, 0, unroll=RET_UNROLL)

    def rec_f(i, st):
        st_sc[i, :, 0:dv] = st.astype(BF16)
        return st * g_f + uf_sc[i]

    def rec_b(t, st):
        i = n - 1 - t
        st_sc[i, :, dv:2 * dv] = st.astype(BF16)
        return st * g_b + ub_sc[i]

    lax.fori_loop(0, n, rec_f, jnp.zeros((dk, dv), F32))
    lax.fori_loop(0, n, rec_b, jnp.zeros((dk, dv), F32))

    def cross(i, carry):
        rows = pl.ds(pl.multiple_of(i * c, c), c)
        cr = jnp.dot(q_ref[rows, :], st_sc[i], preferred_element_type=F32)
        o = oi_sc[rows, :] + cr[:, :dv] * xi_f + cr[:, dv:] * xi_b
        y = o * lax.rsqrt(jnp.mean(o * o, axis=-1, keepdims=True) + NORM_EPS)
        z = jnp.concatenate([z_ref[0, rows, :], z_ref[1, rows, :]], axis=-1)
        out = (y * _silu(z)).astype(o_ref.dtype)
        o_ref[0, rows, :] = out[:, :LANE]
        o_ref[1, rows, :] = out[:, LANE:]
        return carry

    lax.fori_loop(0, n, cross, 0, unroll=RET_UNROLL)


def _retention(qkv, z, dl, z_blk2):
    b, _, s, _ = qkv.shape
    nh = N_HEADS_RET
    dv = 2 * LANE
    n = s // RET_CHUNK
    est = (2 * 2 * s * LANE * 2 + 2 * s * dv * 2 + 2 * s * dv * 4 + 2 * s * dv * 2
           + s * dv * 4 + 2 * n * LANE * dv * 4 + n * LANE * 2 * dv * 2)
    return pl.pallas_call(
        functools.partial(_ret_kernel, seq=s),
        grid=(b, nh),
        in_specs=[
            pl.BlockSpec((None, None, s, LANE), lambda bi, h: (bi, h, 0, 0)),
            pl.BlockSpec((None, None, s, LANE), lambda bi, h: (bi, nh + h, 0, 0)),
            pl.BlockSpec((None, 2, s, LANE), lambda bi, h: (bi, nh + h, 0, 0)),
            pl.BlockSpec((None, 2, s, LANE), lambda bi, h: (bi, z_blk2 + h, 0, 0)),
            pl.BlockSpec((None, 2, 8, dv), lambda bi, h: (h, 0, 0, 0)),
        ],
        out_specs=pl.BlockSpec((None, 2, s, LANE), lambda bi, h: (bi, h, 0, 0)),
        out_shape=jax.ShapeDtypeStruct((b, 2 * nh, s, LANE), BF16),
        scratch_shapes=[
            pltpu.VMEM((s, dv), F32),
            pltpu.VMEM((n, LANE, dv), F32),
            pltpu.VMEM((n, LANE, dv), F32),
            pltpu.VMEM((n, LANE, 2 * dv), BF16),
        ],
        compiler_params=pltpu.CompilerParams(
            dimension_semantics=("arbitrary", "arbitrary"),
            vmem_limit_bytes=_vmem_limit(est)),
        name="retention",
    )(qkv, qkv, qkv, z, dl)


def _outproj_kernel(ya_ref, yr_ref, w_ref, x_ref, mod_ref, gn_ref, modn_ref, *rest, last):
    if last:
        o_ref, wbf_ref = rest
    else:
        xo_ref, h_ref, wbf_ref = rest

    @pl.when((pl.program_id(0) == 0) & (pl.program_id(1) == 0))
    def _():
        _cast_weight(w_ref, wbf_ref)

    y = jnp.concatenate([ya_ref[j] for j in range(ya_ref.shape[0])]
                        + [yr_ref[j] for j in range(yr_ref.shape[0])], axis=-1)
    out = jnp.dot(y, wbf_ref[...], preferred_element_type=F32)
    xn = x_ref[...] + mod_ref[2:3, :] * out
    if last:
        y = xn * lax.rsqrt(jnp.mean(xn * xn, axis=-1, keepdims=True) + NORM_EPS) * gn_ref[...]
        o_ref[...] = y
    else:
        xo_ref[...] = xn
        h = _modulated_norm(xn, gn_ref[...], modn_ref[0:1, :], modn_ref[1:2, :])
        h_ref[...] = h.astype(BF16)


def _outproj(ya, yr, w_out, x, mod, gains, layer, last):
    b, s, d = x.shape
    tm = OUT_ROW_TILE
    nba, nbr = ya.shape[1], yr.shape[1]
    kdim = (nba + nbr) * LANE
    nxt = layer if last else layer + 1
    est = (kdim * d * 4 + kdim * d * 2 + 2 * tm * kdim * 2 + 2 * tm * d * 4
           + 2 * tm * d * 4 + 2 * tm * d * 2 + 3 * tm * d * 4)
    if last:
        out_shape = jax.ShapeDtypeStruct((b, s, d), F32)
        out_specs = pl.BlockSpec((None, tm, d), lambda bi, m: (bi, m, 0))
    else:
        out_shape = (jax.ShapeDtypeStruct((b, s, d), F32), jax.ShapeDtypeStruct((b, s, d), BF16))
        out_specs = (pl.BlockSpec((None, tm, d), lambda bi, m: (bi, m, 0)),
                     pl.BlockSpec((None, tm, d), lambda bi, m: (bi, m, 0)))
    return pl.pallas_call(
        functools.partial(_outproj_kernel, last=last),
        grid=(b, s // tm),
        in_specs=[
            pl.BlockSpec((None, nba, tm, LANE), lambda bi, m: (bi, 0, m, 0)),
            pl.BlockSpec((None, nbr, tm, LANE), lambda bi, m: (bi, 0, m, 0)),
            pl.BlockSpec((None, kdim, d), lambda bi, m: (layer, 0, 0),
                         pipeline_mode=pl.Buffered(1)),
            pl.BlockSpec((None, tm, d), lambda bi, m: (bi, m, 0)),
            pl.BlockSpec((None, None, 3, d), lambda bi, m: (layer, bi, 0, 0)),
            pl.BlockSpec((None, 1, d), lambda bi, m: (layer + 1, 0, 0)),
            pl.BlockSpec((None, None, 3, d), lambda bi, m: (nxt, bi, 0, 0)),
        ],
        out_specs=out_specs,
        out_shape=out_shape,
        scratch_shapes=[pltpu.VMEM((kdim, d), BF16)],
        compiler_params=pltpu.CompilerParams(
            dimension_semantics=("arbitrary", "arbitrary"),
            vmem_limit_bytes=_vmem_limit(est)),
        name="outproj_last" if last else "outproj",
    )(ya, yr, w_out, x, mod, gains, mod)


def kernel(x, c, norm_gain, w_ada, b_ada, w_in, w_out, ret_decay_logit_f, ret_decay_logit_b,
           final_gain):
    b, s, d = x.shape
    depth = w_in.shape[0]
    attn_w = d // 2
    ret_w = d // 2
    n_heads = attn_w // HEAD_DIM_ATTN
    ret_v = ret_w // N_HEADS_RET
    ret_qk = ret_v // 2
    assert ret_qk == LANE and ret_v == 2 * LANE and HEAD_DIM_ATTN == LANE
    assert attn_w == IN_COL_TILE and N_HEADS_RET * ret_qk * 2 == IN_COL_TILE
    assert w_in.shape[2] == 7 * IN_COL_TILE and s % ROW_TILE == 0
    dilations = tuple(dl for _, dl in DILATION_PATTERNS)
    assert all(IN_SUB_ROWS % (dl * 16) == 0 for dl in dilations) and ROW_TILE % IN_SUB_ROWS == 0

    colscale = jnp.ones((7, IN_COL_TILE), F32)
    colscale = colscale.at[0, :].set(HEAD_DIM_ATTN ** -0.5)
    colscale = colscale.at[4, N_HEADS_RET * ret_qk:].set(ret_qk ** -0.5)
    colscale = colscale.reshape(1, 7 * IN_COL_TILE)

    c_pad = jnp.zeros((C_PAD, d), F32).at[:b].set(c)
    mod = _adaln(c_pad, w_ada, b_ada).reshape(depth, C_PAD, 3, d)
    gains = jnp.concatenate([norm_gain, final_gain[None]], axis=0).reshape(depth + 1, 1, d)

    def lanes(v):
        return jnp.broadcast_to(v[:, :, None, None, None], (depth, N_HEADS_RET, 1, 8, 2 * LANE))
    dl = jnp.concatenate([lanes(ret_decay_logit_f), lanes(ret_decay_logit_b)], axis=2)

    h = _modulate(x, gains, mod, 0)
    out = None
    for layer in range(depth):
        qkv_a = _inproj_attn(h, w_in, colscale, layer, 3, lambda j: j, dilations)
        qkv_r = _inproj(h, w_in, colscale, layer, 2, lambda j: j + 4, BF16, "inproj_ret")
        z = _inproj(h, w_in, colscale, layer, 2, lambda j: 3 + 3 * j, F32, "inproj_gate")
        ya = _attention(qkv_a, z, n_heads)
        yr = _retention(qkv_r, z, dl[layer], n_heads // 2)
        last = layer == depth - 1
        res = _outproj(ya, yr, w_out, x, mod, gains, layer, last)
        if last:
            out = res
        else:
            x, h = res
    return out
```

```python
import functools

import jax
import jax.numpy as jnp
from jax import lax
from jax.experimental import pallas as pl
from jax.experimental.pallas import tpu as pltpu

F32 = jnp.float32
BF16 = jnp.bfloat16

HEAD_DIM_ATTN = 128
DILATION_PATTERNS = ((128, 1), (512, 4), (2048, 16))
N_HEADS_RET = 4
RET_CHUNK = 128
NORM_EPS = 1e-6

LANE = 128
VMEM_CAP_BYTES = 60000 * 1024
TEMP_ALLOWANCE = 8 * 2**20

ROW_TILE = 512
OUT_ROW_TILE = 256
IN_COL_TILE = 1024
ADA_COL_TILE = 1024
ATTN_TQ = 128
ATTN_UNROLL = 32
RET_UNROLL = 16
IN_SUB_ROWS = 256
OUT_SUB_ROWS = 128
CAST_ROWS = 256
C_PAD = 16


def _vmem_limit(nbytes):
    return int(min(VMEM_CAP_BYTES, nbytes + TEMP_ALLOWANCE))


def _silu(v):
    return v * jax.nn.sigmoid(v)


def _cast_weight(w_ref, wbf_ref):
    n = w_ref.shape[0] // CAST_ROWS

    def body(i, carry):
        rows = pl.ds(pl.multiple_of(i * CAST_ROWS, CAST_ROWS), CAST_ROWS)
        wbf_ref[rows, :] = w_ref[rows, :].astype(BF16)
        return carry

    lax.fori_loop(0, n, body, 0)


def _ada_kernel(c_ref, w_ref, b_ref, o_ref):
    c_act = _silu(c_ref[...])
    acc = jnp.dot(c_act.astype(BF16), w_ref[...].astype(BF16), preferred_element_type=F32)
    o_ref[...] = acc + b_ref[...]


def _adaln(c_pad, w_ada, b_ada):
    depth, d, n3 = w_ada.shape
    tn = ADA_COL_TILE
    est = 2 * d * tn * 4 + d * tn * 2 + 4 * C_PAD * (d + 2 * tn) * 4
    return pl.pallas_call(
        _ada_kernel,
        grid=(depth, n3 // tn),
        in_specs=[
            pl.BlockSpec((C_PAD, d), lambda l, j: (0, 0)),
            pl.BlockSpec((None, d, tn), lambda l, j: (l, 0, j)),
            pl.BlockSpec((None, 1, tn), lambda l, j: (l, 0, j)),
        ],
        out_specs=pl.BlockSpec((None, C_PAD, tn), lambda l, j: (l, 0, j)),
        out_shape=jax.ShapeDtypeStruct((depth, C_PAD, n3), F32),
        compiler_params=pltpu.CompilerParams(
            dimension_semantics=("arbitrary", "arbitrary"),
            vmem_limit_bytes=_vmem_limit(est)),
        name="adaln_mod",
    )(c_pad, w_ada, b_ada.reshape(depth, 1, n3))


def _modulated_norm(x, g, shift, scale):
    y = x * lax.rsqrt(jnp.mean(x * x, axis=-1, keepdims=True) + NORM_EPS) * g
    return y * (1.0 + scale) + shift


def _modulate_kernel(x_ref, g_ref, mod_ref, h_ref):
    h = _modulated_norm(x_ref[...], g_ref[...], mod_ref[0:1, :], mod_ref[1:2, :])
    h_ref[...] = h.astype(BF16)


def _modulate(x, gains, mod, layer):
    b, s, d = x.shape
    tm = ROW_TILE
    est = 2 * tm * d * (4 + 2) + 3 * tm * d * 4
    return pl.pallas_call(
        _modulate_kernel,
        grid=(b, s // tm),
        in_specs=[
            pl.BlockSpec((None, tm, d), lambda bi, m: (bi, m, 0)),
            pl.BlockSpec((None, 1, d), lambda bi, m: (layer, 0, 0)),
            pl.BlockSpec((None, None, 3, d), lambda bi, m: (layer, bi, 0, 0)),
        ],
        out_specs=pl.BlockSpec((None, tm, d), lambda bi, m: (bi, m, 0)),
        out_shape=jax.ShapeDtypeStruct((b, s, d), BF16),
        compiler_params=pltpu.CompilerParams(
            dimension_semantics=("arbitrary", "arbitrary"),
            vmem_limit_bytes=_vmem_limit(est)),
        name="modulate0",
    )(x, gains, mod)


def _project_tile(h_ref, w_ref, cs_ref, wbf_ref):
    @pl.when((pl.program_id(1) == 0) & (pl.program_id(2) == 0))
    def _():
        _cast_weight(w_ref, wbf_ref)

    acc = jnp.dot(h_ref[...], wbf_ref[...], preferred_element_type=F32)
    return acc * cs_ref[...]


def _inproj_kernel(h_ref, w_ref, cs_ref, o_ref, wbf_ref):
    acc = _project_tile(h_ref, w_ref, cs_ref, wbf_ref)
    for j in range(o_ref.shape[0]):
        o_ref[j] = acc[:, j * LANE:(j + 1) * LANE].astype(o_ref.dtype)


def _inproj_attn_kernel(h_ref, w_ref, cs_ref, *rest, dilations):
    nd = len(dilations)
    out_refs = rest[:nd]
    wbf_ref = rest[nd]
    stage_refs = rest[nd + 1:]

    @pl.when((pl.program_id(1) == 0) & (pl.program_id(2) == 0))
    def _():
        _cast_weight(w_ref, wbf_ref)

    nb, sub = stage_refs[0].shape[0], stage_refs[0].shape[2]
    for t in range(len(stage_refs) // (nd - 1)):
        stages = stage_refs[t * (nd - 1):(t + 1) * (nd - 1)]
        s0 = t * sub
        acc = jnp.dot(h_ref[s0:s0 + sub, :], wbf_ref[...], preferred_element_type=F32)
        acc = acc * cs_ref[...]
        for j in range(nb):
            blk = acc[:, j * LANE:(j + 1) * LANE]
            stages[0][j, 0] = blk
            out_refs[0][j, 0, s0:s0 + sub, :] = blk.astype(out_refs[0].dtype)
        for k in range(1, nd):
            prev, dil = dilations[k - 1], dilations[k]
            q, n = dil // prev, sub // dil
            for j in range(nb):
                for r in range(dil):
                    ra, rb = r % prev, r // prev
                    v = stages[k - 1][j, ra, pl.ds(rb, n, stride=q), :]
                    out_refs[k][j, r, s0 // dil:s0 // dil + n, :] = v.astype(out_refs[k].dtype)
                    if k < nd - 1:
                        stages[k][j, r] = v


def _inproj_specs(h, layer, tile_of):
    b, s, d = h.shape
    tm, tn = ROW_TILE, IN_COL_TILE
    return [
        pl.BlockSpec((None, tm, d), lambda j, bi, m: (bi, m, 0)),
        pl.BlockSpec((None, d, tn), lambda j, bi, m: (layer, 0, tile_of(j))),
        pl.BlockSpec((1, tn), lambda j, bi, m: (0, tile_of(j))),
    ]


def _inproj(h, w_in, colscale, layer, n_tiles, tile_of, out_dtype, name):
    b, s, d = h.shape
    tm, tn = ROW_TILE, IN_COL_TILE
    nb = tn // LANE
    osz = jnp.dtype(out_dtype).itemsize
    est = 2 * d * tn * 4 + d * tn * 2 + 2 * tm * d * 2 + 2 * tm * tn * osz + 2 * tm * tn * 4
    return pl.pallas_call(
        _inproj_kernel,
        grid=(n_tiles, b, s // tm),
        in_specs=_inproj_specs(h, layer, tile_of),
        out_specs=pl.BlockSpec((None, nb, tm, LANE), lambda j, bi, m: (bi, j, m, 0)),
        out_shape=jax.ShapeDtypeStruct((b, n_tiles * nb, s, LANE), out_dtype),
        scratch_shapes=[pltpu.VMEM((d, tn), BF16)],
        compiler_params=pltpu.CompilerParams(
            dimension_semantics=("arbitrary", "arbitrary", "arbitrary"),
            vmem_limit_bytes=_vmem_limit(est)),
        name=name,
    )(h, w_in, colscale)


def _inproj_attn(h, w_in, colscale, layer, n_tiles, tile_of, dilations):
    b, s, d = h.shape
    tm, tn = ROW_TILE, IN_COL_TILE
    nb = tn // LANE
    nd = len(dilations)
    est = (2 * d * tn * 4 + d * tn * 2 + 2 * tm * d * 2 + nd * 2 * tm * tn * 2
           + 3 * tm * tn * 4)
    return pl.pallas_call(
        functools.partial(_inproj_attn_kernel, dilations=dilations),
        grid=(n_tiles, b, s // tm),
        in_specs=_inproj_specs(h, layer, tile_of),
        out_specs=[pl.BlockSpec((None, nb, dl, tm // dl, LANE),
                                lambda j, bi, m: (bi, j, 0, m, 0)) for dl in dilations],
        out_shape=[jax.ShapeDtypeStruct((b, n_tiles * nb, dl, s // dl, LANE), BF16)
                   for dl in dilations],
        scratch_shapes=[pltpu.VMEM((d, tn), BF16)]
        + [pltpu.VMEM((nb, dl, IN_SUB_ROWS // dl, LANE), F32)
           for dl in dilations[:-1]] * (tm // IN_SUB_ROWS),
        compiler_params=pltpu.CompilerParams(
            dimension_semantics=("arbitrary", "arbitrary", "arbitrary"),
            vmem_limit_bytes=_vmem_limit(est)),
        name="inproj_attn",
    )(h, w_in, colscale)


def _attn_kernel(*refs, n_heads, patterns, seq):
    npat = len(patterns)
    qkv_refs = refs[:3 * npat]
    z_ref, o_ref, acc_sc, lse_sc, bias_sc = refs[3 * npat:]
    tq = ATTN_TQ
    radius = patterns[0][0] // (2 * patterns[0][1])
    tk = tq + 2 * radius

    head = pl.program_id(1)
    expo = (head + 1).astype(F32) * (-8.0 / n_heads)
    slope = jnp.exp2(jnp.full((tq, tk), 1.0, F32) * expo)
    row = lax.broadcasted_iota(jnp.int32, (tq, tk), 0)
    col = lax.broadcasted_iota(jnp.int32, (tq, tk), 1)
    for p, (_, dil) in enumerate(patterns):
        for case in range(3):
            rel = jnp.abs(col - row - case * radius)
            bias = -slope * (rel * dil).astype(F32)
            bias_sc[p * 3 + case] = jnp.where(rel <= radius, bias, -jnp.inf)

    nblocks = seq // tq
    for p, (_, dil) in enumerate(patterns):
        sub_len = seq // dil
        nblk = sub_len // tq
        shift = nblk.bit_length() - 1
        q_ref, k_ref, v_ref = qkv_refs[3 * p:3 * p + 3]

        def block(g, carry, p=p, dil=dil, sub_len=sub_len, nblk=nblk, shift=shift,
                  q_ref=q_ref, k_ref=k_ref, v_ref=v_ref):
            r = lax.shift_right_logical(g, shift)
            i = lax.bitwise_and(g, nblk - 1)
            l0 = pl.multiple_of(i * tq, tq)
            ks = pl.multiple_of(jnp.clip(l0 - radius, 0, sub_len - tk), radius)
            case = (l0 - ks) // radius
            q = q_ref[r, pl.ds(l0, tq), :]
            k = k_ref[r, pl.ds(ks, tk), :]
            v = v_ref[r, pl.ds(ks, tk), :]
            s = lax.dot_general(q, k, (((1,), (1,)), ((), ())), preferred_element_type=F32)
            s = s + bias_sc[p * 3 + case]
            m = jnp.max(s, axis=-1, keepdims=True)
            pe = jnp.exp(s - m)
            den = jnp.sum(pe, axis=-1, keepdims=True)
            acc = jnp.dot(pe.astype(BF16), v, preferred_element_type=F32) / den
            lse = m + jnp.log(den)
            if dil == 1:
                rows = pl.ds(l0, tq)
            else:
                rows = pl.ds(l0 * dil + r, tq, stride=dil)
            acc_sc[p, rows, :] = acc
            lse_sc[p, rows, :] = jnp.broadcast_to(lse, (tq, LANE))
            return carry

        lax.fori_loop(0, nblocks, block, 0, unroll=ATTN_UNROLL)

    ch = 256

    def merge(i, carry):
        rows = pl.ds(pl.multiple_of(i * ch, ch), ch)
        ls = [lse_sc[p, rows, :] for p in range(npat)]
        mx = functools.reduce(jnp.maximum, ls)
        num = jnp.zeros((ch, LANE), F32)
        den = jnp.zeros((ch, LANE), F32)
        for p in range(npat):
            e = jnp.exp(ls[p] - mx)
            num = num + e * acc_sc[p, rows, :]
            den = den + e
        o_ref[rows, :] = ((num / den) * _silu(z_ref[rows, :])).astype(o_ref.dtype)
        return carry

    lax.fori_loop(0, seq // ch, merge, 0)


def _attention(qkv_views, z, n_heads):
    patterns = DILATION_PATTERNS
    b, _, _, s, _ = qkv_views[0].shape
    radius = patterns[0][0] // (2 * patterns[0][1])
    assert all(w // (2 * dl) == radius for w, dl in patterns)
    tk = ATTN_TQ + 2 * radius
    for _, dl in patterns:
        nblk = (s // dl) // ATTN_TQ
        assert (s // dl) % ATTN_TQ == 0 and s // dl >= tk and nblk & (nblk - 1) == 0
    assert (s // ATTN_TQ) % ATTN_UNROLL == 0
    npat = len(patterns)

    operands, in_specs = [], []
    for (_, dl), view in zip(patterns, qkv_views):
        for t in range(3):
            operands.append(view)
            in_specs.append(pl.BlockSpec(
                (None, None, dl, s // dl, LANE),
                lambda bi, h, t=t: (bi, t * n_heads + h, 0, 0, 0)))
    operands.append(z)
    in_specs.append(pl.BlockSpec((None, None, s, LANE), lambda bi, h: (bi, h, 0, 0)))

    est = (2 * 3 * npat * s * LANE * 2 + 2 * s * LANE * 4 + 2 * s * LANE * 2
           + 2 * npat * s * LANE * 4 + 3 * npat * ATTN_TQ * tk * 4)
    return pl.pallas_call(
        functools.partial(_attn_kernel, n_heads=n_heads, patterns=patterns, seq=s),
        grid=(b, n_heads),
        in_specs=in_specs,
        out_specs=pl.BlockSpec((None, None, s, LANE), lambda bi, h: (bi, h, 0, 0)),
        out_shape=jax.ShapeDtypeStruct((b, n_heads, s, LANE), BF16),
        scratch_shapes=[
            pltpu.VMEM((npat, s, LANE), F32),
            pltpu.VMEM((npat, s, LANE), F32),
            pltpu.VMEM((3 * npat, ATTN_TQ, tk), F32),
        ],
        compiler_params=pltpu.CompilerParams(
            dimension_semantics=("arbitrary", "arbitrary"),
            vmem_limit_bytes=_vmem_limit(est)),
        name="dilated_attention",
    )(*operands)


def _log_sigmoid(v):
    return jnp.minimum(v, 0.0) - jnp.log1p(jnp.exp(-jnp.abs(v)))


def _ret_kernel(q_ref, k_ref, v_ref, z_ref, dl_ref, o_ref, oi_sc, uf_sc, ub_sc, st_sc, *, seq):
    c = RET_CHUNK
    n = seq // c
    dk = q_ref.shape[-1]
    dv = 2 * LANE

    lg_f = _log_sigmoid(dl_ref[0, 0:1, :])
    lg_b = _log_sigmoid(dl_ref[1, 0:1, :])
    ri = lax.broadcasted_iota(jnp.int32, (c, c), 0)
    ci = lax.broadcasted_iota(jnp.int32, (c, c), 1)
    rel = (ri - ci).astype(F32)
    dec = (jnp.where(rel >= 0, jnp.exp(jnp.maximum(rel, 0.0) * lg_f[:, :c]), 0.0)
           + jnp.where(rel <= 0, jnp.exp(jnp.maximum(-rel, 0.0) * lg_b[:, :c]), 0.0))
    idx_v = lax.broadcasted_iota(jnp.int32, (c, dv), 0).astype(F32)
    idx_k = lax.broadcasted_iota(jnp.int32, (c, dk), 0).astype(F32)
    xi_f = jnp.exp((idx_v + 1.0) * lg_f)
    xi_b = jnp.exp((c - idx_v) * lg_b)
    zeta_f = jnp.exp((c - 1.0 - idx_k) * lg_f[:, :dk])
    zeta_b = jnp.exp(idx_k * lg_b[:, :dk])
    g_f = jnp.exp(c * lg_f)
    g_b = jnp.exp(c * lg_b)

    def intra(i, carry):
        rows = pl.ds(pl.multiple_of(i * c, c), c)
        q = q_ref[rows, :]
        k = k_ref[rows, :]
        v = jnp.concatenate([v_ref[0, rows, :], v_ref[1, rows, :]], axis=-1)
        qk = lax.dot_general(q, k, (((1,), (1,)), ((), ())), preferred_element_type=F32)
        kf = k.astype(F32)
        lhs = jnp.concatenate([(qk * dec).astype(BF16),
                               (kf * zeta_f).T.astype(BF16),
                               (kf * zeta_b).T.astype(BF16)], axis=0)
        res = jnp.dot(lhs, v, preferred_element_type=F32)
        oi_sc[rows, :] = res[:c]
        uf_sc[i] = res[c:c + dk]
        ub_sc[i] = res[c + dk:]
        return carry

    lax.fori_loop(0, n, intra, 0, unroll=RET_UNROLL)

    def rec_f(i, st):
        st_sc[i, :, 0:dv] = st.astype(BF16)
        return st * g_f + uf_sc[i]

    def rec_b(t, st):
        i = n - 1 - t
        st_sc[i, :, dv:2 * dv] = st.astype(BF16)
        return st * g_b + ub_sc[i]

    lax.fori_loop(0, n, rec_f, jnp.zeros((dk, dv), F32))
    lax.fori_loop(0, n, rec_b, jnp.zeros((dk, dv), F32))

    def cross(i, carry):
        rows = pl.ds(pl.multiple_of(i * c, c), c)
        cr = jnp.dot(q_ref[rows, :], st_sc[i], preferred_element_type=F32)
        o = oi_sc[rows, :] + cr[:, :dv] * xi_f + cr[:, dv:] * xi_b
        y = o * lax.rsqrt(jnp.mean(o * o, axis=-1, keepdims=True) + NORM_EPS)
        z = jnp.concatenate([z_ref[0, rows, :], z_ref[1, rows, :]], axis=-1)
        out = (y * _silu(z)).astype(o_ref.dtype)
        o_ref[0, rows, :] = out[:, :LANE]
        o_ref[1, rows, :] = out[:, LANE:]
        return carry

    lax.fori_loop(0, n, cross, 0, unroll=RET_UNROLL)


def _retention(qkv, z, dl, z_blk2):
    b, _, s, _ = qkv.shape
    nh = N_HEADS_RET
    dv = 2 * LANE
    n = s // RET_CHUNK
    est = (2 * 2 * s * LANE * 2 + 2 * s * dv * 2 + 2 * s * dv * 4 + 2 * s * dv * 2
           + s * dv * 4 + 2 * n * LANE * dv * 4 + n * LANE * 2 * dv * 2)
    return pl.pallas_call(
        functools.partial(_ret_kernel, seq=s),
        grid=(b, nh),
        in_specs=[
            pl.BlockSpec((None, None, s, LANE), lambda bi, h: (bi, h, 0, 0)),
            pl.BlockSpec((None, None, s, LANE), lambda bi, h: (bi, nh + h, 0, 0)),
            pl.BlockSpec((None, 2, s, LANE), lambda bi, h: (bi, nh + h, 0, 0)),
            pl.BlockSpec((None, 2, s, LANE), lambda bi, h: (bi, z_blk2 + h, 0, 0)),
            pl.BlockSpec((None, 2, 8, dv), lambda bi, h: (h, 0, 0, 0)),
        ],
        out_specs=pl.BlockSpec((None, 2, s, LANE), lambda bi, h: (bi, h, 0, 0)),
        out_shape=jax.ShapeDtypeStruct((b, 2 * nh, s, LANE), BF16),
        scratch_shapes=[
            pltpu.VMEM((s, dv), F32),
            pltpu.VMEM((n, LANE, dv), F32),
            pltpu.VMEM((n, LANE, dv), F32),
            pltpu.VMEM((n, LANE, 2 * dv), BF16),
        ],
        compiler_params=pltpu.CompilerParams(
            dimension_semantics=("arbitrary", "arbitrary"),
            vmem_limit_bytes=_vmem_limit(est)),
        name="retention",
    )(qkv, qkv, qkv, z, dl)


def _outproj_kernel(ya_ref, yr_ref, w_ref, x_ref, mod_ref, gn_ref, modn_ref, *rest, last):
    if last:
        o_ref, wbf_ref = rest
    else:
        xo_ref, h_ref, wbf_ref = rest

    @pl.when((pl.program_id(0) == 0) & (pl.program_id(1) == 0))
    def _():
        _cast_weight(w_ref, wbf_ref)

    tm = x_ref.shape[0]
    for s0 in range(0, tm, OUT_SUB_ROWS):
        rows = slice(s0, s0 + OUT_SUB_ROWS)
        y = jnp.concatenate([ya_ref[j, rows, :] for j in range(ya_ref.shape[0])]
                            + [yr_ref[j, rows, :] for j in range(yr_ref.shape[0])], axis=-1)
        out = jnp.dot(y, wbf_ref[...], preferred_element_type=F32)
        xn = x_ref[rows, :] + mod_ref[2:3, :] * out
        if last:
            o_ref[rows, :] = (xn * lax.rsqrt(jnp.mean(xn * xn, axis=-1, keepdims=True) + NORM_EPS)
                              * gn_ref[...])
        else:
            xo_ref[rows, :] = xn
            h = _modulated_norm(xn, gn_ref[...], modn_ref[0:1, :], modn_ref[1:2, :])
            h_ref[rows, :] = h.astype(BF16)


def _outproj(ya, yr, w_out, x, mod, gains, layer, last):
    b, s, d = x.shape
    tm = OUT_ROW_TILE
    nba, nbr = ya.shape[1], yr.shape[1]
    kdim = (nba + nbr) * LANE
    nxt = layer if last else layer + 1
    est = (kdim * d * 4 + kdim * d * 2 + 2 * tm * kdim * 2 + 2 * tm * d * 4
           + 2 * tm * d * 4 + 2 * tm * d * 2 + 3 * tm * d * 4)
    if last:
        out_shape = jax.ShapeDtypeStruct((b, s, d), F32)
        out_specs = pl.BlockSpec((None, tm, d), lambda bi, m: (bi, m, 0))
    else:
        out_shape = (jax.ShapeDtypeStruct((b, s, d), F32), jax.ShapeDtypeStruct((b, s, d), BF16))
        out_specs = (pl.BlockSpec((None, tm, d), lambda bi, m: (bi, m, 0)),
                     pl.BlockSpec((None, tm, d), lambda bi, m: (bi, m, 0)))
    return pl.pallas_call(
        functools.partial(_outproj_kernel, last=last),
        grid=(b, s // tm),
        in_specs=[
            pl.BlockSpec((None, nba, tm, LANE), lambda bi, m: (bi, 0, m, 0)),
            pl.BlockSpec((None, nbr, tm, LANE), lambda bi, m: (bi, 0, m, 0)),
            pl.BlockSpec((None, kdim, d), lambda bi, m: (layer, 0, 0),
                         pipeline_mode=pl.Buffered(1)),
            pl.BlockSpec((None, tm, d), lambda bi, m: (bi, m, 0)),
            pl.BlockSpec((None, None, 3, d), lambda bi, m: (layer, bi, 0, 0)),
            pl.BlockSpec((None, 1, d), lambda bi, m: (layer + 1, 0, 0)),
            pl.BlockSpec((None, None, 3, d), lambda bi, m: (nxt, bi, 0, 0)),
        ],
        out_specs=out_specs,
        out_shape=out_shape,
        scratch_shapes=[pltpu.VMEM((kdim, d), BF16)],
        compiler_params=pltpu.CompilerParams(
            dimension_semantics=("arbitrary", "arbitrary"),
            vmem_limit_bytes=_vmem_limit(est)),
        name="outproj_last" if last else "outproj",
    )(ya, yr, w_out, x, mod, gains, mod)


def kernel(x, c, norm_gain, w_ada, b_ada, w_in, w_out, ret_decay_logit_f, ret_decay_logit_b,
           final_gain):
    b, s, d = x.shape
    depth = w_in.shape[0]
    attn_w = d // 2
    ret_w = d // 2
    n_heads = attn_w // HEAD_DIM_ATTN
    ret_v = ret_w // N_HEADS_RET
    ret_qk = ret_v // 2
    assert ret_qk == LANE and ret_v == 2 * LANE and HEAD_DIM_ATTN == LANE
    assert attn_w == IN_COL_TILE and N_HEADS_RET * ret_qk * 2 == IN_COL_TILE
    assert w_in.shape[2] == 7 * IN_COL_TILE and s % ROW_TILE == 0
    dilations = tuple(dl for _, dl in DILATION_PATTERNS)
    assert all(IN_SUB_ROWS % (dl * 16) == 0 for dl in dilations) and ROW_TILE % IN_SUB_ROWS == 0
    assert dilations[0] == 1 and all(b_ % a_ == 0 for a_, b_ in zip(dilations, dilations[1:]))

    colscale = jnp.ones((7, IN_COL_TILE), F32)
    colscale = colscale.at[0, :].set(HEAD_DIM_ATTN ** -0.5)
    colscale = colscale.at[4, N_HEADS_RET * ret_qk:].set(ret_qk ** -0.5)
    colscale = colscale.reshape(1, 7 * IN_COL_TILE)

    c_pad = jnp.zeros((C_PAD, d), F32).at[:b].set(c)
    mod = _adaln(c_pad, w_ada, b_ada).reshape(depth, C_PAD, 3, d)
    gains = jnp.concatenate([norm_gain, final_gain[None]], axis=0).reshape(depth + 1, 1, d)

    def lanes(v):
        return jnp.broadcast_to(v[:, :, None, None, None], (depth, N_HEADS_RET, 1, 8, 2 * LANE))
    dl = jnp.concatenate([lanes(ret_decay_logit_f), lanes(ret_decay_logit_b)], axis=2)

    h = _modulate(x, gains, mod, 0)
    out = None
    for layer in range(depth):
        qkv_a = _inproj_attn(h, w_in, colscale, layer, 3, lambda j: j, dilations)
        qkv_r = _inproj(h, w_in, colscale, layer, 2, lambda j: j + 4, BF16, "inproj_ret")
        z = _inproj(h, w_in, colscale, layer, 2, lambda j: 3 + 3 * j, F32, "inproj_gate")
        ya = _attention(qkv_a, z, n_heads)
        yr = _retention(qkv_r, z, dl[layer], n_heads // 2)
        last = layer == depth - 1
        res = _outproj(ya, yr, w_out, x, mod, gains, layer, last)
        if last:
            out = res
        else:
            x, h = res
    return out
```

```python
import functools

import jax
import jax.numpy as jnp
from jax import lax
from jax.experimental import pallas as pl
from jax.experimental.pallas import tpu as pltpu

F32 = jnp.float32
BF16 = jnp.bfloat16

HEAD_DIM_ATTN = 128
DILATION_PATTERNS = ((128, 1), (512, 4), (2048, 16))
N_HEADS_RET = 4
RET_CHUNK = 128
NORM_EPS = 1e-6
LOG2_E = 1.4426950408889634

LANE = 128
VMEM_CAP_BYTES = 60000 * 1024
TEMP_ALLOWANCE = 8 * 2**20

ROW_TILE = 1024
NORM_ROW_TILE = 512
OUT_ROW_TILE = 256
IN_COL_TILE = 1024
ADA_COL_TILE = 1024
ATTN_TQ = 128
ATTN_UNROLL = 32
RET_UNROLL = 16
IN_SUB_ROWS = 256
OUT_SUB_ROWS = 128
CAST_ROWS = 256
C_PAD = 16


def _vmem_limit(nbytes):
    return int(min(VMEM_CAP_BYTES, nbytes + TEMP_ALLOWANCE))


def _silu(v):
    return v * jax.nn.sigmoid(v)


def _cast_weight(w_ref, wbf_ref):
    n = w_ref.shape[0] // CAST_ROWS

    def body(i, carry):
        rows = pl.ds(pl.multiple_of(i * CAST_ROWS, CAST_ROWS), CAST_ROWS)
        wbf_ref[rows, :] = w_ref[rows, :].astype(BF16)
        return carry

    lax.fori_loop(0, n, body, 0)


def _ada_kernel(c_ref, w_ref, b_ref, o_ref):
    c_act = _silu(c_ref[...])
    acc = jnp.dot(c_act.astype(BF16), w_ref[...].astype(BF16), preferred_element_type=F32)
    o_ref[...] = acc + b_ref[...]


def _adaln(c_pad, w_ada, b_ada):
    depth, d, n3 = w_ada.shape
    tn = ADA_COL_TILE
    est = 2 * d * tn * 4 + d * tn * 2 + 4 * C_PAD * (d + 2 * tn) * 4
    return pl.pallas_call(
        _ada_kernel,
        grid=(depth, n3 // tn),
        in_specs=[
            pl.BlockSpec((C_PAD, d), lambda l, j: (0, 0)),
            pl.BlockSpec((None, d, tn), lambda l, j: (l, 0, j)),
            pl.BlockSpec((None, 1, tn), lambda l, j: (l, 0, j)),
        ],
        out_specs=pl.BlockSpec((None, C_PAD, tn), lambda l, j: (l, 0, j)),
        out_shape=jax.ShapeDtypeStruct((depth, C_PAD, n3), F32),
        compiler_params=pltpu.CompilerParams(
            dimension_semantics=("arbitrary", "arbitrary"),
            vmem_limit_bytes=_vmem_limit(est)),
        name="adaln_mod",
    )(c_pad, w_ada, b_ada.reshape(depth, 1, n3))


def _modulated_norm(x, g, shift, scale):
    y = x * lax.rsqrt(jnp.mean(x * x, axis=-1, keepdims=True) + NORM_EPS) * g
    return y * (1.0 + scale) + shift


def _modulate_kernel(x_ref, g_ref, mod_ref, h_ref):
    h = _modulated_norm(x_ref[...], g_ref[...], mod_ref[0:1, :], mod_ref[1:2, :])
    h_ref[...] = h.astype(BF16)


def _modulate(x, gains, mod, layer):
    b, s, d = x.shape
    tm = NORM_ROW_TILE
    est = 2 * tm * d * (4 + 2) + 3 * tm * d * 4
    return pl.pallas_call(
        _modulate_kernel,
        grid=(b, s // tm),
        in_specs=[
            pl.BlockSpec((None, tm, d), lambda bi, m: (bi, m, 0)),
            pl.BlockSpec((None, 1, d), lambda bi, m: (layer, 0, 0)),
            pl.BlockSpec((None, None, 3, d), lambda bi, m: (layer, bi, 0, 0)),
        ],
        out_specs=pl.BlockSpec((None, tm, d), lambda bi, m: (bi, m, 0)),
        out_shape=jax.ShapeDtypeStruct((b, s, d), BF16),
        compiler_params=pltpu.CompilerParams(
            dimension_semantics=("arbitrary", "arbitrary"),
            vmem_limit_bytes=_vmem_limit(est)),
        name="modulate0",
    )(x, gains, mod)


def _project_tile(h_ref, w_ref, cs_ref, wbf_ref):
    @pl.when((pl.program_id(1) == 0) & (pl.program_id(2) == 0))
    def _():
        _cast_weight(w_ref, wbf_ref)

    acc = jnp.dot(h_ref[...], wbf_ref[...], preferred_element_type=F32)
    return acc * cs_ref[...]


def _inproj_kernel(h_ref, w_ref, cs_ref, o_ref, wbf_ref):
    acc = _project_tile(h_ref, w_ref, cs_ref, wbf_ref)
    for j in range(o_ref.shape[0]):
        o_ref[j] = acc[:, j * LANE:(j + 1) * LANE].astype(o_ref.dtype)


def _inproj_attn_kernel(h_ref, w_ref, cs_ref, *rest, dilations):
    nd = len(dilations)
    out_refs = rest[:nd]
    wbf_ref = rest[nd]
    stage_refs = rest[nd + 1:]

    @pl.when((pl.program_id(1) == 0) & (pl.program_id(2) == 0))
    def _():
        _cast_weight(w_ref, wbf_ref)

    nb, sub = stage_refs[0].shape[0], stage_refs[0].shape[2]
    for t in range(len(stage_refs) // (nd - 1)):
        stages = stage_refs[t * (nd - 1):(t + 1) * (nd - 1)]
        s0 = t * sub
        acc = jnp.dot(h_ref[s0:s0 + sub, :], wbf_ref[...], preferred_element_type=F32)
        acc = acc * cs_ref[...]
        for j in range(nb):
            blk = acc[:, j * LANE:(j + 1) * LANE]
            stages[0][j, 0] = blk
            out_refs[0][j, 0, s0:s0 + sub, :] = blk.astype(out_refs[0].dtype)
        for k in range(1, nd):
            prev, dil = dilations[k - 1], dilations[k]
            q, n = dil // prev, sub // dil
            for j in range(nb):
                for r in range(dil):
                    ra, rb = r % prev, r // prev
                    v = stages[k - 1][j, ra, pl.ds(rb, n, stride=q), :]
                    out_refs[k][j, r, s0 // dil:s0 // dil + n, :] = v.astype(out_refs[k].dtype)
                    if k < nd - 1:
                        stages[k][j, r] = v


def _inproj_specs(h, layer, tile_of):
    b, s, d = h.shape
    tm, tn = ROW_TILE, IN_COL_TILE
    return [
        pl.BlockSpec((None, tm, d), lambda j, bi, m: (bi, m, 0)),
        pl.BlockSpec((None, d, tn), lambda j, bi, m: (layer, 0, tile_of(j))),
        pl.BlockSpec((1, tn), lambda j, bi, m: (0, tile_of(j))),
    ]


def _inproj(h, w_in, colscale, layer, n_tiles, tile_of, out_dtype, name):
    b, s, d = h.shape
    tm, tn = ROW_TILE, IN_COL_TILE
    nb = tn // LANE
    osz = jnp.dtype(out_dtype).itemsize
    est = 2 * d * tn * 4 + d * tn * 2 + 2 * tm * d * 2 + 2 * tm * tn * osz + 2 * tm * tn * 4
    return pl.pallas_call(
        _inproj_kernel,
        grid=(n_tiles, b, s // tm),
        in_specs=_inproj_specs(h, layer, tile_of),
        out_specs=pl.BlockSpec((None, nb, tm, LANE), lambda j, bi, m: (bi, j, m, 0)),
        out_shape=jax.ShapeDtypeStruct((b, n_tiles * nb, s, LANE), out_dtype),
        scratch_shapes=[pltpu.VMEM((d, tn), BF16)],
        compiler_params=pltpu.CompilerParams(
            dimension_semantics=("arbitrary", "arbitrary", "arbitrary"),
            vmem_limit_bytes=_vmem_limit(est)),
        name=name,
    )(h, w_in, colscale)


def _inproj_attn(h, w_in, colscale, layer, n_tiles, tile_of, dilations):
    b, s, d = h.shape
    tm, tn = ROW_TILE, IN_COL_TILE
    nb = tn // LANE
    nd = len(dilations)
    est = (2 * d * tn * 4 + d * tn * 2 + 2 * tm * d * 2 + nd * 2 * tm * tn * 2
           + 3 * tm * tn * 4)
    return pl.pallas_call(
        functools.partial(_inproj_attn_kernel, dilations=dilations),
        grid=(n_tiles, b, s // tm),
        in_specs=_inproj_specs(h, layer, tile_of),
        out_specs=[pl.BlockSpec((None, nb, dl, tm // dl, LANE),
                                lambda j, bi, m: (bi, j, 0, m, 0)) for dl in dilations],
        out_shape=[jax.ShapeDtypeStruct((b, n_tiles * nb, dl, s // dl, LANE), BF16)
                   for dl in dilations],
        scratch_shapes=[pltpu.VMEM((d, tn), BF16)]
        + [pltpu.VMEM((nb, dl, IN_SUB_ROWS // dl, LANE), F32)
           for dl in dilations[:-1]] * (tm // IN_SUB_ROWS),
        compiler_params=pltpu.CompilerParams(
            dimension_semantics=("arbitrary", "arbitrary", "arbitrary"),
            vmem_limit_bytes=_vmem_limit(est)),
        name="inproj_attn",
    )(h, w_in, colscale)


def _attn_kernel(*refs, n_heads, patterns, seq):
    npat = len(patterns)
    qkv_refs = refs[:3 * npat]
    z_ref, o_ref, acc_sc, lse_sc, bias_sc = refs[3 * npat:]
    tq = ATTN_TQ
    radius = patterns[0][0] // (2 * patterns[0][1])
    tk = tq + 2 * radius

    head = pl.program_id(1)
    expo = (head + 1).astype(F32) * (-8.0 / n_heads)
    slope = jnp.exp2(jnp.full((tq, tk), 1.0, F32) * expo)
    row = lax.broadcasted_iota(jnp.int32, (tq, tk), 0)
    col = lax.broadcasted_iota(jnp.int32, (tq, tk), 1)
    for p, (_, dil) in enumerate(patterns):
        for case in range(3):
            rel = jnp.abs(col - row - case * radius)
            bias = -slope * (rel * dil).astype(F32) * LOG2_E
            bias_sc[p * 3 + case] = jnp.where(rel <= radius, bias, -jnp.inf)

    nblocks = seq // tq
    strided = [p for p in range(npat) if patterns[p][1] != 1]
    slot_of = {p: i for i, p in enumerate(strided)}
    for p in strided + [p for p in range(npat) if patterns[p][1] == 1]:
        dil = patterns[p][1]
        sub_len = seq // dil
        nblk = sub_len // tq
        shift = nblk.bit_length() - 1
        q_ref, k_ref, v_ref = qkv_refs[3 * p:3 * p + 3]

        def block(g, carry, p=p, dil=dil, sub_len=sub_len, nblk=nblk, shift=shift,
                  q_ref=q_ref, k_ref=k_ref, v_ref=v_ref):
            r = lax.shift_right_logical(g, shift)
            i = lax.bitwise_and(g, nblk - 1)
            l0 = pl.multiple_of(i * tq, tq)
            ks = pl.multiple_of(jnp.clip(l0 - radius, 0, sub_len - tk), radius)
            case = (l0 - ks) // radius
            q = q_ref[r, pl.ds(l0, tq), :]
            k = k_ref[r, pl.ds(ks, tk), :]
            v = v_ref[r, pl.ds(ks, tk), :]
            s = lax.dot_general(q, k, (((1,), (1,)), ((), ())), preferred_element_type=F32)
            s = s + bias_sc[p * 3 + case]
            m = jnp.max(s, axis=-1, keepdims=True)
            pe = jnp.exp2(s - m)
            den = jnp.sum(pe, axis=-1, keepdims=True)
            acc = jnp.dot(pe.astype(BF16), v, preferred_element_type=F32) / den
            lse = m + jnp.log2(den)
            if dil != 1:
                rows = pl.ds(l0 * dil + r, tq, stride=dil)
                acc_sc[slot_of[p], rows, :] = acc
                lse_sc[slot_of[p], rows, :] = jnp.broadcast_to(lse, (tq, LANE))
                return carry
            rows = pl.ds(l0, tq)
            others = [lse_sc[i, rows, :] for i in range(len(strided))]
            mx = jnp.maximum(functools.reduce(jnp.maximum, others), lse)
            e = jnp.exp2(lse - mx)
            num = e * acc
            wsum = e
            for i, ls in enumerate(others):
                e = jnp.exp2(ls - mx)
                num = num + e * acc_sc[i, rows, :]
                wsum = wsum + e
            o_ref[rows, :] = ((num / wsum) * _silu(z_ref[rows, :])).astype(o_ref.dtype)
            return carry

        lax.fori_loop(0, nblocks, block, 0, unroll=ATTN_UNROLL)


def _attention(qkv_views, z, n_heads):
    patterns = DILATION_PATTERNS
    b, _, _, s, _ = qkv_views[0].shape
    radius = patterns[0][0] // (2 * patterns[0][1])
    assert all(w // (2 * dl) == radius for w, dl in patterns)
    tk = ATTN_TQ + 2 * radius
    for _, dl in patterns:
        nblk = (s // dl) // ATTN_TQ
        assert (s // dl) % ATTN_TQ == 0 and s // dl >= tk and nblk & (nblk - 1) == 0
    assert (s // ATTN_TQ) % ATTN_UNROLL == 0
    npat = len(patterns)

    operands, in_specs = [], []
    for (_, dl), view in zip(patterns, qkv_views):
        for t in range(3):
            operands.append(view)
            in_specs.append(pl.BlockSpec(
                (None, None, dl, s // dl, LANE),
                lambda bi, h, t=t: (bi, t * n_heads + h, 0, 0, 0)))
    operands.append(z)
    in_specs.append(pl.BlockSpec((None, None, s, LANE), lambda bi, h: (bi, h, 0, 0)))

    assert sum(dl == 1 for _, dl in patterns) == 1 and npat > 1
    est = (2 * 3 * npat * s * LANE * 2 + 2 * s * LANE * 4 + 2 * s * LANE * 2
           + 2 * (npat - 1) * s * LANE * 4 + 3 * npat * ATTN_TQ * tk * 4)
    return pl.pallas_call(
        functools.partial(_attn_kernel, n_heads=n_heads, patterns=patterns, seq=s),
        grid=(b, n_heads),
        in_specs=in_specs,
        out_specs=pl.BlockSpec((None, None, s, LANE), lambda bi, h: (bi, h, 0, 0)),
        out_shape=jax.ShapeDtypeStruct((b, n_heads, s, LANE), BF16),
        scratch_shapes=[
            pltpu.VMEM((npat - 1, s, LANE), F32),
            pltpu.VMEM((npat - 1, s, LANE), F32),
            pltpu.VMEM((3 * npat, ATTN_TQ, tk), F32),
        ],
        compiler_params=pltpu.CompilerParams(
            dimension_semantics=("arbitrary", "arbitrary"),
            vmem_limit_bytes=_vmem_limit(est)),
        name="dilated_attention",
    )(*operands)


def _log_sigmoid(v):
    return jnp.minimum(v, 0.0) - jnp.log1p(jnp.exp(-jnp.abs(v)))


def _ret_kernel(q_ref, k_ref, v_ref, z_ref, dl_ref, o_ref, oi_sc, uf_sc, ub_sc, st_sc, *, seq):
    c = RET_CHUNK
    n = seq // c
    dk = q_ref.shape[-1]
    dv = 2 * LANE

    lg_f = _log_sigmoid(dl_ref[0, 0:1, :])
    lg_b = _log_sigmoid(dl_ref[1, 0:1, :])
    ri = lax.broadcasted_iota(jnp.int32, (c, c), 0)
    ci = lax.broadcasted_iota(jnp.int32, (c, c), 1)
    rel = (ri - ci).astype(F32)
    dec = (jnp.where(rel >= 0, jnp.exp(jnp.maximum(rel, 0.0) * lg_f[:, :c]), 0.0)
           + jnp.where(rel <= 0, jnp.exp(jnp.maximum(-rel, 0.0) * lg_b[:, :c]), 0.0))
    idx_v = lax.broadcasted_iota(jnp.int32, (c, dv), 0).astype(F32)
    idx_k = lax.broadcasted_iota(jnp.int32, (c, dk), 0).astype(F32)
    xi_f = jnp.exp((idx_v + 1.0) * lg_f)
    xi_b = jnp.exp((c - idx_v) * lg_b)
    zeta_f = jnp.exp((c - 1.0 - idx_k) * lg_f[:, :dk])
    zeta_b = jnp.exp(idx_k * lg_b[:, :dk])
    g_f = jnp.exp(c * lg_f)
    g_b = jnp.exp(c * lg_b)

    def intra(i, carry):
        rows = pl.ds(pl.multiple_of(i * c, c), c)
        q = q_ref[rows, :]
        k = k_ref[rows, :]
        v = jnp.concatenate([v_ref[0, rows, :], v_ref[1, rows, :]], axis=-1)
        qk = lax.dot_general(q, k, (((1,), (1,)), ((), ())), preferred_element_type=F32)
        kf = k.astype(F32)
        lhs = jnp.concatenate([(qk * dec).astype(BF16),
                               (kf * zeta_f).T.astype(BF16),
                               (kf * zeta_b).T.astype(BF16)], axis=0)
        res = jnp.dot(lhs, v, preferred_element_type=F32)
        oi_sc[rows, :] = res[:c]
        uf_sc[i] = res[c:c + dk]
        ub_sc[i] = res[c + dk:]
        return carry

    lax.fori_loop(0, n, intra, 0, unroll=RET_UNROLL)

    def rec_f(i, st):
        st_sc[i, :, 0:dv] = st.astype(BF16)
        return st * g_f + uf_sc[i]

    def rec_b(t, st):
        i = n - 1 - t
        st_sc[i, :, dv:2 * dv] = st.astype(BF16)
        return st * g_b + ub_sc[i]

    lax.fori_loop(0, n, rec_f, jnp.zeros((dk, dv), F32))
    lax.fori_loop(0, n, rec_b, jnp.zeros((dk, dv), F32))

    def cross(i, carry):
        rows = pl.ds(pl.multiple_of(i * c, c), c)
        cr = jnp.dot(q_ref[rows, :], st_sc[i], preferred_element_type=F32)
        o = oi_sc[rows, :] + cr[:, :dv] * xi_f + cr[:, dv:] * xi_b
        y = o * lax.rsqrt(jnp.mean(o * o, axis=-1, keepdims=True) + NORM_EPS)
        z = jnp.concatenate([z_ref[0, rows, :], z_ref[1, rows, :]], axis=-1)
        out = (y * _silu(z)).astype(o_ref.dtype)
        o_ref[0, rows, :] = out[:, :LANE]
        o_ref[1, rows, :] = out[:, LANE:]
        return carry

    lax.fori_loop(0, n, cross, 0, unroll=RET_UNROLL)


def _retention(qkv, z, dl, z_blk2):
    b, _, s, _ = qkv.shape
    nh = N_HEADS_RET
    dv = 2 * LANE
    n = s // RET_CHUNK
    est = (2 * 2 * s * LANE * 2 + 2 * s * dv * 2 + 2 * s * dv * 4 + 2 * s * dv * 2
           + s * dv * 4 + 2 * n * LANE * dv * 4 + n * LANE * 2 * dv * 2)
    return pl.pallas_call(
        functools.partial(_ret_kernel, seq=s),
        grid=(b, nh),
        in_specs=[
            pl.BlockSpec((None, None, s, LANE), lambda bi, h: (bi, h, 0, 0)),
            pl.BlockSpec((None, None, s, LANE), lambda bi, h: (bi, nh + h, 0, 0)),
            pl.BlockSpec((None, 2, s, LANE), lambda bi, h: (bi, nh + h, 0, 0)),
            pl.BlockSpec((None, 2, s, LANE), lambda bi, h: (bi, z_blk2 + h, 0, 0)),
            pl.BlockSpec((None, 2, 8, dv), lambda bi, h: (h, 0, 0, 0)),
        ],
        out_specs=pl.BlockSpec((None, 2, s, LANE), lambda bi, h: (bi, h, 0, 0)),
        out_shape=jax.ShapeDtypeStruct((b, 2 * nh, s, LANE), BF16),
        scratch_shapes=[
            pltpu.VMEM((s, dv), F32),
            pltpu.VMEM((n, LANE, dv), F32),
            pltpu.VMEM((n, LANE, dv), F32),
            pltpu.VMEM((n, LANE, 2 * dv), BF16),
        ],
        compiler_params=pltpu.CompilerParams(
            dimension_semantics=("arbitrary", "arbitrary"),
            vmem_limit_bytes=_vmem_limit(est)),
        name="retention",
    )(qkv, qkv, qkv, z, dl)


def _outproj_kernel(ya_ref, yr_ref, w_ref, x_ref, mod_ref, gn_ref, modn_ref, *rest, last):
    if last:
        o_ref, wbf_ref = rest
    else:
        xo_ref, h_ref, wbf_ref = rest

    @pl.when((pl.program_id(0) == 0) & (pl.program_id(1) == 0))
    def _():
        _cast_weight(w_ref, wbf_ref)

    tm = x_ref.shape[0]
    for s0 in range(0, tm, OUT_SUB_ROWS):
        rows = slice(s0, s0 + OUT_SUB_ROWS)
        y = jnp.concatenate([ya_ref[j, rows, :] for j in range(ya_ref.shape[0])]
                            + [yr_ref[j, rows, :] for j in range(yr_ref.shape[0])], axis=-1)
        out = jnp.dot(y, wbf_ref[...], preferred_element_type=F32)
        xn = x_ref[rows, :] + mod_ref[2:3, :] * out
        if last:
            o_ref[rows, :] = (xn * lax.rsqrt(jnp.mean(xn * xn, axis=-1, keepdims=True) + NORM_EPS)
                              * gn_ref[...])
        else:
            xo_ref[rows, :] = xn
            h = _modulated_norm(xn, gn_ref[...], modn_ref[0:1, :], modn_ref[1:2, :])
            h_ref[rows, :] = h.astype(BF16)


def _outproj(ya, yr, w_out, x, mod, gains, layer, last):
    b, s, d = x.shape
    tm = OUT_ROW_TILE
    nba, nbr = ya.shape[1], yr.shape[1]
    kdim = (nba + nbr) * LANE
    nxt = layer if last else layer + 1
    est = (kdim * d * 4 + kdim * d * 2 + 2 * tm * kdim * 2 + 2 * tm * d * 4
           + 2 * tm * d * 4 + 2 * tm * d * 2 + 3 * tm * d * 4)
    if last:
        out_shape = jax.ShapeDtypeStruct((b, s, d), F32)
        out_specs = pl.BlockSpec((None, tm, d), lambda bi, m: (bi, m, 0))
    else:
        out_shape = (jax.ShapeDtypeStruct((b, s, d), F32), jax.ShapeDtypeStruct((b, s, d), BF16))
        out_specs = (pl.BlockSpec((None, tm, d), lambda bi, m: (bi, m, 0)),
                     pl.BlockSpec((None, tm, d), lambda bi, m: (bi, m, 0)))
    return pl.pallas_call(
        functools.partial(_outproj_kernel, last=last),
        grid=(b, s // tm),
        in_specs=[
            pl.BlockSpec((None, nba, tm, LANE), lambda bi, m: (bi, 0, m, 0)),
            pl.BlockSpec((None, nbr, tm, LANE), lambda bi, m: (bi, 0, m, 0)),
            pl.BlockSpec((None, kdim, d), lambda bi, m: (layer, 0, 0),
                         pipeline_mode=pl.Buffered(1)),
            pl.BlockSpec((None, tm, d), lambda bi, m: (bi, m, 0)),
            pl.BlockSpec((None, None, 3, d), lambda bi, m: (layer, bi, 0, 0)),
            pl.BlockSpec((None, 1, d), lambda bi, m: (layer + 1, 0, 0)),
            pl.BlockSpec((None, None, 3, d), lambda bi, m: (nxt, bi, 0, 0)),
        ],
        out_specs=out_specs,
        out_shape=out_shape,
        scratch_shapes=[pltpu.VMEM((kdim, d), BF16)],
        compiler_params=pltpu.CompilerParams(
            dimension_semantics=("arbitrary", "arbitrary"),
            vmem_limit_bytes=_vmem_limit(est)),
        name="outproj_last" if last else "outproj",
    )(ya, yr, w_out, x, mod, gains, mod)


def kernel(x, c, norm_gain, w_ada, b_ada, w_in, w_out, ret_decay_logit_f, ret_decay_logit_b,
           final_gain):
    b, s, d = x.shape
    depth = w_in.shape[0]
    attn_w = d // 2
    ret_w = d // 2
    n_heads = attn_w // HEAD_DIM_ATTN
    ret_v = ret_w // N_HEADS_RET
    ret_qk = ret_v // 2
    assert ret_qk == LANE and ret_v == 2 * LANE and HEAD_DIM_ATTN == LANE
    assert attn_w == IN_COL_TILE and N_HEADS_RET * ret_qk * 2 == IN_COL_TILE
    assert w_in.shape[2] == 7 * IN_COL_TILE and s % ROW_TILE == 0
    dilations = tuple(dl for _, dl in DILATION_PATTERNS)
    assert all(IN_SUB_ROWS % (dl * 16) == 0 for dl in dilations) and ROW_TILE % IN_SUB_ROWS == 0
    assert dilations[0] == 1 and all(b_ % a_ == 0 for a_, b_ in zip(dilations, dilations[1:]))

    colscale = jnp.ones((7, IN_COL_TILE), F32)
    colscale = colscale.at[0, :].set(HEAD_DIM_ATTN ** -0.5 * LOG2_E)
    colscale = colscale.at[4, N_HEADS_RET * ret_qk:].set(ret_qk ** -0.5)
    colscale = colscale.reshape(1, 7 * IN_COL_TILE)

    c_pad = jnp.zeros((C_PAD, d), F32).at[:b].set(c)
    mod = _adaln(c_pad, w_ada, b_ada).reshape(depth, C_PAD, 3, d)
    gains = jnp.concatenate([norm_gain, final_gain[None]], axis=0).reshape(depth + 1, 1, d)

    def lanes(v):
        return jnp.broadcast_to(v[:, :, None, None, None], (depth, N_HEADS_RET, 1, 8, 2 * LANE))
    dl = jnp.concatenate([lanes(ret_decay_logit_f), lanes(ret_decay_logit_b)], axis=2)

    h = _modulate(x, gains, mod, 0)
    out = None
    for layer in range(depth):
        qkv_a = _inproj_attn(h, w_in, colscale, layer, 3, lambda j: j, dilations)
        qkv_r = _inproj(h, w_in, colscale, layer, 2, lambda j: j + 4, BF16, "inproj_ret")
        z = _inproj(h, w_in, colscale, layer, 2, lambda j: 3 + 3 * j, F32, "inproj_gate")
        ya = _attention(qkv_a, z, n_heads)
        yr = _retention(qkv_r, z, dl[layer], n_heads // 2)
        last = layer == depth - 1
        res = _outproj(ya, yr, w_out, x, mod, gains, layer, last)
        if last:
            out = res
        else:
            x, h = res
    return out
```

```python
import functools

import jax
import jax.numpy as jnp
from jax import lax
from jax.experimental import pallas as pl
from jax.experimental.pallas import tpu as pltpu

F32 = jnp.float32
BF16 = jnp.bfloat16

HEAD_DIM_ATTN = 128
DILATION_PATTERNS = ((128, 1), (512, 4), (2048, 16))
N_HEADS_RET = 4
RET_CHUNK = 128
NORM_EPS = 1e-6
LOG2_E = 1.4426950408889634

LANE = 128
VMEM_CAP_BYTES = 60000 * 1024
TEMP_ALLOWANCE = 8 * 2**20

ROW_TILE = 1024
NORM_ROW_TILE = 512
OUT_ROW_TILE = 256
IN_COL_TILE = 1024
ADA_COL_TILE = 1024
ATTN_TQ = 128
ATTN_UNROLL = 32
RET_UNROLL = 32
IN_SUB_ROWS = 256
OUT_SUB_ROWS = 128
CAST_ROWS = 256
C_PAD = 16


def _vmem_limit(nbytes):
    return int(min(VMEM_CAP_BYTES, nbytes + TEMP_ALLOWANCE))


def _silu(v):
    return v * jax.nn.sigmoid(v)


def _cast_weight(w_ref, wbf_ref):
    n = w_ref.shape[0] // CAST_ROWS

    def body(i, carry):
        rows = pl.ds(pl.multiple_of(i * CAST_ROWS, CAST_ROWS), CAST_ROWS)
        wbf_ref[rows, :] = w_ref[rows, :].astype(BF16)
        return carry

    lax.fori_loop(0, n, body, 0)


def _ada_kernel(c_ref, w_ref, b_ref, o_ref):
    c_act = _silu(c_ref[...])
    acc = jnp.dot(c_act.astype(BF16), w_ref[...].astype(BF16), preferred_element_type=F32)
    o_ref[...] = acc + b_ref[...]


def _adaln(c_pad, w_ada, b_ada):
    depth, d, n3 = w_ada.shape
    tn = ADA_COL_TILE
    est = 2 * d * tn * 4 + d * tn * 2 + 4 * C_PAD * (d + 2 * tn) * 4
    return pl.pallas_call(
        _ada_kernel,
        grid=(depth, n3 // tn),
        in_specs=[
            pl.BlockSpec((C_PAD, d), lambda l, j: (0, 0)),
            pl.BlockSpec((None, d, tn), lambda l, j: (l, 0, j)),
            pl.BlockSpec((None, 1, tn), lambda l, j: (l, 0, j)),
        ],
        out_specs=pl.BlockSpec((None, C_PAD, tn), lambda l, j: (l, 0, j)),
        out_shape=jax.ShapeDtypeStruct((depth, C_PAD, n3), F32),
        compiler_params=pltpu.CompilerParams(
            dimension_semantics=("arbitrary", "arbitrary"),
            vmem_limit_bytes=_vmem_limit(est)),
        name="adaln_mod",
    )(c_pad, w_ada, b_ada.reshape(depth, 1, n3))


def _modulated_norm(x, g, shift, scale):
    y = x * lax.rsqrt(jnp.mean(x * x, axis=-1, keepdims=True) + NORM_EPS)
    return y * (g * (1.0 + scale)) + shift


def _modulate_kernel(x_ref, g_ref, mod_ref, h_ref):
    h = _modulated_norm(x_ref[...], g_ref[...], mod_ref[0:1, :], mod_ref[1:2, :])
    h_ref[...] = h.astype(BF16)


def _modulate(x, gains, mod, layer):
    b, s, d = x.shape
    tm = NORM_ROW_TILE
    est = 2 * tm * d * (4 + 2) + 3 * tm * d * 4
    return pl.pallas_call(
        _modulate_kernel,
        grid=(b, s // tm),
        in_specs=[
            pl.BlockSpec((None, tm, d), lambda bi, m: (bi, m, 0)),
            pl.BlockSpec((None, 1, d), lambda bi, m: (layer, 0, 0)),
            pl.BlockSpec((None, None, 3, d), lambda bi, m: (layer, bi, 0, 0)),
        ],
        out_specs=pl.BlockSpec((None, tm, d), lambda bi, m: (bi, m, 0)),
        out_shape=jax.ShapeDtypeStruct((b, s, d), BF16),
        compiler_params=pltpu.CompilerParams(
            dimension_semantics=("arbitrary", "arbitrary"),
            vmem_limit_bytes=_vmem_limit(est)),
        name="modulate0",
    )(x, gains, mod)


def _project_tile(h_ref, w_ref, cs_ref, wbf_ref):
    @pl.when((pl.program_id(1) == 0) & (pl.program_id(2) == 0))
    def _():
        _cast_weight(w_ref, wbf_ref)

    acc = jnp.dot(h_ref[...], wbf_ref[...], preferred_element_type=F32)
    return acc * cs_ref[...]


def _inproj_kernel(h_ref, w_ref, cs_ref, o_ref, wbf_ref):
    acc = _project_tile(h_ref, w_ref, cs_ref, wbf_ref)
    for j in range(o_ref.shape[0]):
        o_ref[j] = acc[:, j * LANE:(j + 1) * LANE].astype(o_ref.dtype)


def _inproj_attn_kernel(h_ref, w_ref, cs_ref, *rest, dilations):
    nd = len(dilations)
    out_refs = rest[:nd]
    wbf_ref = rest[nd]
    stage_refs = rest[nd + 1:]

    @pl.when((pl.program_id(1) == 0) & (pl.program_id(2) == 0))
    def _():
        _cast_weight(w_ref, wbf_ref)

    nb, sub = stage_refs[0].shape[0], stage_refs[0].shape[2]
    for t in range(len(stage_refs) // (nd - 1)):
        stages = stage_refs[t * (nd - 1):(t + 1) * (nd - 1)]
        s0 = t * sub
        acc = jnp.dot(h_ref[s0:s0 + sub, :], wbf_ref[...], preferred_element_type=F32)
        acc = acc * cs_ref[...]
        for j in range(nb):
            blk = acc[:, j * LANE:(j + 1) * LANE]
            stages[0][j, 0] = blk
            out_refs[0][j, 0, s0:s0 + sub, :] = blk.astype(out_refs[0].dtype)
        for k in range(1, nd):
            prev, dil = dilations[k - 1], dilations[k]
            q, n = dil // prev, sub // dil
            for j in range(nb):
                for r in range(dil):
                    ra, rb = r % prev, r // prev
                    v = stages[k - 1][j, ra, pl.ds(rb, n, stride=q), :]
                    out_refs[k][j, r, s0 // dil:s0 // dil + n, :] = v.astype(out_refs[k].dtype)
                    if k < nd - 1:
                        stages[k][j, r] = v


def _inproj_specs(h, layer, tile_of):
    b, s, d = h.shape
    tm, tn = ROW_TILE, IN_COL_TILE
    return [
        pl.BlockSpec((None, tm, d), lambda j, bi, m: (bi, m, 0)),
        pl.BlockSpec((None, d, tn), lambda j, bi, m: (layer, 0, tile_of(j))),
        pl.BlockSpec((1, tn), lambda j, bi, m: (0, tile_of(j))),
    ]


def _inproj(h, w_in, colscale, layer, n_tiles, tile_of, out_dtype, name):
    b, s, d = h.shape
    tm, tn = ROW_TILE, IN_COL_TILE
    nb = tn // LANE
    osz = jnp.dtype(out_dtype).itemsize
    est = 2 * d * tn * 4 + d * tn * 2 + 2 * tm * d * 2 + 2 * tm * tn * osz + 2 * tm * tn * 4
    return pl.pallas_call(
        _inproj_kernel,
        grid=(n_tiles, b, s // tm),
        in_specs=_inproj_specs(h, layer, tile_of),
        out_specs=pl.BlockSpec((None, nb, tm, LANE), lambda j, bi, m: (bi, j, m, 0)),
        out_shape=jax.ShapeDtypeStruct((b, n_tiles * nb, s, LANE), out_dtype),
        scratch_shapes=[pltpu.VMEM((d, tn), BF16)],
        compiler_params=pltpu.CompilerParams(
            dimension_semantics=("arbitrary", "arbitrary", "arbitrary"),
            vmem_limit_bytes=_vmem_limit(est)),
        name=name,
    )(h, w_in, colscale)


def _inproj_attn(h, w_in, colscale, layer, n_tiles, tile_of, dilations):
    b, s, d = h.shape
    tm, tn = ROW_TILE, IN_COL_TILE
    nb = tn // LANE
    nd = len(dilations)
    est = (2 * d * tn * 4 + d * tn * 2 + 2 * tm * d * 2 + nd * 2 * tm * tn * 2
           + 3 * tm * tn * 4)
    return pl.pallas_call(
        functools.partial(_inproj_attn_kernel, dilations=dilations),
        grid=(n_tiles, b, s // tm),
        in_specs=_inproj_specs(h, layer, tile_of),
        out_specs=[pl.BlockSpec((None, nb, dl, tm // dl, LANE),
                                lambda j, bi, m: (bi, j, 0, m, 0)) for dl in dilations],
        out_shape=[jax.ShapeDtypeStruct((b, n_tiles * nb, dl, s // dl, LANE), BF16)
                   for dl in dilations],
        scratch_shapes=[pltpu.VMEM((d, tn), BF16)]
        + [pltpu.VMEM((nb, dl, IN_SUB_ROWS // dl, LANE), F32)
           for dl in dilations[:-1]] * (tm // IN_SUB_ROWS),
        compiler_params=pltpu.CompilerParams(
            dimension_semantics=("arbitrary", "arbitrary", "arbitrary"),
            vmem_limit_bytes=_vmem_limit(est)),
        name="inproj_attn",
    )(h, w_in, colscale)


def _attn_kernel(*refs, n_heads, patterns, seq):
    npat = len(patterns)
    qkv_refs = refs[:3 * npat]
    z_ref, o_ref, acc_sc, m_sc, den_sc, bias_sc = refs[3 * npat:]
    tq = ATTN_TQ
    radius = patterns[0][0] // (2 * patterns[0][1])
    tk = tq + 2 * radius

    head = pl.program_id(1)
    expo = (head + 1).astype(F32) * (-8.0 / n_heads)
    slope = jnp.exp2(jnp.full((tq, tk), 1.0, F32) * expo)
    row = lax.broadcasted_iota(jnp.int32, (tq, tk), 0)
    col = lax.broadcasted_iota(jnp.int32, (tq, tk), 1)
    for p, (_, dil) in enumerate(patterns):
        for case in range(3):
            rel = jnp.abs(col - row - case * radius)
            bias = -slope * (rel * dil).astype(F32) * LOG2_E
            bias_sc[p * 3 + case] = jnp.where(rel <= radius, bias, -jnp.inf)

    nblocks = seq // tq
    strided = [p for p in range(npat) if patterns[p][1] != 1]
    slot_of = {p: i for i, p in enumerate(strided)}
    for p in strided + [p for p in range(npat) if patterns[p][1] == 1]:
        dil = patterns[p][1]
        sub_len = seq // dil
        nblk = sub_len // tq
        shift = nblk.bit_length() - 1
        q_ref, k_ref, v_ref = qkv_refs[3 * p:3 * p + 3]

        def block(g, carry, p=p, dil=dil, sub_len=sub_len, nblk=nblk, shift=shift,
                  q_ref=q_ref, k_ref=k_ref, v_ref=v_ref):
            r = lax.shift_right_logical(g, shift)
            i = lax.bitwise_and(g, nblk - 1)
            l0 = pl.multiple_of(i * tq, tq)
            ks = pl.multiple_of(jnp.clip(l0 - radius, 0, sub_len - tk), radius)
            case = (l0 - ks) // radius
            q = q_ref[r, pl.ds(l0, tq), :]
            k = k_ref[r, pl.ds(ks, tk), :]
            v = v_ref[r, pl.ds(ks, tk), :]
            s = lax.dot_general(q, k, (((1,), (1,)), ((), ())), preferred_element_type=F32)
            s = s + bias_sc[p * 3 + case]
            m = jnp.max(s, axis=-1, keepdims=True)
            pe = jnp.exp2(s - m)
            den = jnp.sum(pe, axis=-1, keepdims=True)
            acc = jnp.dot(pe.astype(BF16), v, preferred_element_type=F32)
            if dil != 1:
                rows = pl.ds(l0 * dil + r, tq, stride=dil)
                acc_sc[slot_of[p], rows, :] = acc
                m_sc[slot_of[p], rows, :] = jnp.broadcast_to(m, (tq, LANE))
                den_sc[slot_of[p], rows, :] = jnp.broadcast_to(den, (tq, LANE))
                return carry
            rows = pl.ds(l0, tq)
            others = [m_sc[i, rows, :] for i in range(len(strided))]
            mx = jnp.maximum(functools.reduce(jnp.maximum, others), m)
            e = jnp.exp2(m - mx)
            num = e * acc
            wsum = e * den
            for i, ms in enumerate(others):
                e = jnp.exp2(ms - mx)
                num = num + e * acc_sc[i, rows, :]
                wsum = wsum + e * den_sc[i, rows, :]
            o_ref[rows, :] = ((num / wsum) * _silu(z_ref[rows, :])).astype(o_ref.dtype)
            return carry

        lax.fori_loop(0, nblocks, block, 0, unroll=ATTN_UNROLL)


def _attention(qkv_views, z, n_heads):
    patterns = DILATION_PATTERNS
    b, _, _, s, _ = qkv_views[0].shape
    radius = patterns[0][0] // (2 * patterns[0][1])
    assert all(w // (2 * dl) == radius for w, dl in patterns)
    tk = ATTN_TQ + 2 * radius
    for _, dl in patterns:
        nblk = (s // dl) // ATTN_TQ
        assert (s // dl) % ATTN_TQ == 0 and s // dl >= tk and nblk & (nblk - 1) == 0
    assert (s // ATTN_TQ) % ATTN_UNROLL == 0
    npat = len(patterns)

    operands, in_specs = [], []
    for (_, dl), view in zip(patterns, qkv_views):
        for t in range(3):
            operands.append(view)
            in_specs.append(pl.BlockSpec(
                (None, None, dl, s // dl, LANE),
                lambda bi, h, t=t: (bi, t * n_heads + h, 0, 0, 0)))
    operands.append(z)
    in_specs.append(pl.BlockSpec((None, None, s, LANE), lambda bi, h: (bi, h, 0, 0)))

    assert sum(dl == 1 for _, dl in patterns) == 1 and npat > 1
    est = (2 * 3 * npat * s * LANE * 2 + 2 * s * LANE * 4 + 2 * s * LANE * 2
           + 3 * (npat - 1) * s * LANE * 4 + 3 * npat * ATTN_TQ * tk * 4)
    return pl.pallas_call(
        functools.partial(_attn_kernel, n_heads=n_heads, patterns=patterns, seq=s),
        grid=(b, n_heads),
        in_specs=in_specs,
        out_specs=pl.BlockSpec((None, None, s, LANE), lambda bi, h: (bi, h, 0, 0)),
        out_shape=jax.ShapeDtypeStruct((b, n_heads, s, LANE), BF16),
        scratch_shapes=[
            pltpu.VMEM((npat - 1, s, LANE), F32),
            pltpu.VMEM((npat - 1, s, LANE), F32),
            pltpu.VMEM((npat - 1, s, LANE), F32),
            pltpu.VMEM((3 * npat, ATTN_TQ, tk), F32),
        ],
        compiler_params=pltpu.CompilerParams(
            dimension_semantics=("arbitrary", "arbitrary"),
            vmem_limit_bytes=_vmem_limit(est)),
        name="dilated_attention",
    )(*operands)


def _log_sigmoid(v):
    return jnp.minimum(v, 0.0) - jnp.log1p(jnp.exp(-jnp.abs(v)))


def _ret_kernel(q_ref, k_ref, v_ref, z_ref, dl_ref, o_ref, oi_sc, uf_sc, ub_sc, st_sc, *, seq):
    c = RET_CHUNK
    n = seq // c
    dk = q_ref.shape[-1]
    dv = 2 * LANE

    lg_f = _log_sigmoid(dl_ref[0, 0:1, :])
    lg_b = _log_sigmoid(dl_ref[1, 0:1, :])
    ri = lax.broadcasted_iota(jnp.int32, (c, c), 0)
    ci = lax.broadcasted_iota(jnp.int32, (c, c), 1)
    rel = (ri - ci).astype(F32)
    dec = (jnp.where(rel >= 0, jnp.exp(jnp.maximum(rel, 0.0) * lg_f[:, :c]), 0.0)
           + jnp.where(rel <= 0, jnp.exp(jnp.maximum(-rel, 0.0) * lg_b[:, :c]), 0.0))
    idx_v = lax.broadcasted_iota(jnp.int32, (c, dv), 0).astype(F32)
    idx_k = lax.broadcasted_iota(jnp.int32, (c, dk), 0).astype(F32)
    xi_f = jnp.exp((idx_v + 1.0) * lg_f)
    xi_b = jnp.exp((c - idx_v) * lg_b)
    zeta_f = jnp.exp((c - 1.0 - idx_k) * lg_f[:, :dk])
    zeta_b = jnp.exp(idx_k * lg_b[:, :dk])
    g_f = jnp.exp(c * lg_f)
    g_b = jnp.exp(c * lg_b)

    def intra(i, carry):
        rows = pl.ds(pl.multiple_of(i * c, c), c)
        q = q_ref[rows, :]
        k = k_ref[rows, :]
        v = jnp.concatenate([v_ref[0, rows, :], v_ref[1, rows, :]], axis=-1)
        qk = lax.dot_general(q, k, (((1,), (1,)), ((), ())), preferred_element_type=F32)
        kf = k.astype(F32)
        lhs = jnp.concatenate([(qk * dec).astype(BF16),
                               (kf * zeta_f).T.astype(BF16),
                               (kf * zeta_b).T.astype(BF16)], axis=0)
        res = jnp.dot(lhs, v, preferred_element_type=F32)
        oi_sc[rows, :] = res[:c]
        uf_sc[i] = res[c:c + dk]
        ub_sc[i] = res[c + dk:]
        return carry

    lax.fori_loop(0, n, intra, 0, unroll=RET_UNROLL)

    def rec_f(i, st):
        st_sc[i, :, 0:dv] = st.astype(BF16)
        return st * g_f + uf_sc[i]

    def rec_b(t, st):
        i = n - 1 - t
        st_sc[i, :, dv:2 * dv] = st.astype(BF16)
        return st * g_b + ub_sc[i]

    lax.fori_loop(0, n, rec_f, jnp.zeros((dk, dv), F32))
    lax.fori_loop(0, n, rec_b, jnp.zeros((dk, dv), F32))

    def cross(i, carry):
        rows = pl.ds(pl.multiple_of(i * c, c), c)
        cr = jnp.dot(q_ref[rows, :], st_sc[i], preferred_element_type=F32)
        o = oi_sc[rows, :] + cr[:, :dv] * xi_f + cr[:, dv:] * xi_b
        y = o * lax.rsqrt(jnp.mean(o * o, axis=-1, keepdims=True) + NORM_EPS)
        z = jnp.concatenate([z_ref[0, rows, :], z_ref[1, rows, :]], axis=-1)
        out = (y * _silu(z)).astype(o_ref.dtype)
        o_ref[0, rows, :] = out[:, :LANE]
        o_ref[1, rows, :] = out[:, LANE:]
        return carry

    lax.fori_loop(0, n, cross, 0, unroll=RET_UNROLL)


def _retention(qkv, z, dl, z_blk2):
    b, _, s, _ = qkv.shape
    nh = N_HEADS_RET
    dv = 2 * LANE
    n = s // RET_CHUNK
    est = (2 * 2 * s * LANE * 2 + 2 * s * dv * 2 + 2 * s * dv * 4 + 2 * s * dv * 2
           + s * dv * 4 + 2 * n * LANE * dv * 4 + n * LANE * 2 * dv * 2)
    return pl.pallas_call(
        functools.partial(_ret_kernel, seq=s),
        grid=(b, nh),
        in_specs=[
            pl.BlockSpec((None, None, s, LANE), lambda bi, h: (bi, h, 0, 0)),
            pl.BlockSpec((None, None, s, LANE), lambda bi, h: (bi, nh + h, 0, 0)),
            pl.BlockSpec((None, 2, s, LANE), lambda bi, h: (bi, nh + h, 0, 0)),
            pl.BlockSpec((None, 2, s, LANE), lambda bi, h: (bi, z_blk2 + h, 0, 0)),
            pl.BlockSpec((None, 2, 8, dv), lambda bi, h: (h, 0, 0, 0)),
        ],
        out_specs=pl.BlockSpec((None, 2, s, LANE), lambda bi, h: (bi, h, 0, 0)),
        out_shape=jax.ShapeDtypeStruct((b, 2 * nh, s, LANE), BF16),
        scratch_shapes=[
            pltpu.VMEM((s, dv), F32),
            pltpu.VMEM((n, LANE, dv), F32),
            pltpu.VMEM((n, LANE, dv), F32),
            pltpu.VMEM((n, LANE, 2 * dv), BF16),
        ],
        compiler_params=pltpu.CompilerParams(
            dimension_semantics=("arbitrary", "arbitrary"),
            vmem_limit_bytes=_vmem_limit(est)),
        name="retention",
    )(qkv, qkv, qkv, z, dl)


def _outproj_kernel(ya_ref, yr_ref, w_ref, x_ref, mod_ref, gn_ref, modn_ref, *rest, last):
    if last:
        o_ref, wbf_ref = rest
    else:
        xo_ref, h_ref, wbf_ref = rest

    @pl.when((pl.program_id(0) == 0) & (pl.program_id(1) == 0))
    def _():
        _cast_weight(w_ref, wbf_ref)

    tm = x_ref.shape[0]
    for s0 in range(0, tm, OUT_SUB_ROWS):
        rows = slice(s0, s0 + OUT_SUB_ROWS)
        y = jnp.concatenate([ya_ref[j, rows, :] for j in range(ya_ref.shape[0])]
                            + [yr_ref[j, rows, :] for j in range(yr_ref.shape[0])], axis=-1)
        out = jnp.dot(y, wbf_ref[...], preferred_element_type=F32)
        xn = x_ref[rows, :] + mod_ref[2:3, :] * out
        if last:
            o_ref[rows, :] = (xn * lax.rsqrt(jnp.mean(xn * xn, axis=-1, keepdims=True) + NORM_EPS)
                              * gn_ref[...])
        else:
            xo_ref[rows, :] = xn
            h = _modulated_norm(xn, gn_ref[...], modn_ref[0:1, :], modn_ref[1:2, :])
            h_ref[rows, :] = h.astype(BF16)


def _outproj(ya, yr, w_out, x, mod, gains, layer, last):
    b, s, d = x.shape
    tm = OUT_ROW_TILE
    nba, nbr = ya.shape[1], yr.shape[1]
    kdim = (nba + nbr) * LANE
    nxt = layer if last else layer + 1
    est = (kdim * d * 4 + kdim * d * 2 + 2 * tm * kdim * 2 + 2 * tm * d * 4
           + 2 * tm * d * 4 + 2 * tm * d * 2 + 3 * tm * d * 4)
    if last:
        out_shape = jax.ShapeDtypeStruct((b, s, d), F32)
        out_specs = pl.BlockSpec((None, tm, d), lambda bi, m: (bi, m, 0))
    else:
        out_shape = (jax.ShapeDtypeStruct((b, s, d), F32), jax.ShapeDtypeStruct((b, s, d), BF16))
        out_specs = (pl.BlockSpec((None, tm, d), lambda bi, m: (bi, m, 0)),
                     pl.BlockSpec((None, tm, d), lambda bi, m: (bi, m, 0)))
    return pl.pallas_call(
        functools.partial(_outproj_kernel, last=last),
        grid=(b, s // tm),
        in_specs=[
            pl.BlockSpec((None, nba, tm, LANE), lambda bi, m: (bi, 0, m, 0)),
            pl.BlockSpec((None, nbr, tm, LANE), lambda bi, m: (bi, 0, m, 0)),
            pl.BlockSpec((None, kdim, d), lambda bi, m: (layer, 0, 0),
                         pipeline_mode=pl.Buffered(1)),
            pl.BlockSpec((None, tm, d), lambda bi, m: (bi, m, 0)),
            pl.BlockSpec((None, None, 3, d), lambda bi, m: (layer, bi, 0, 0)),
            pl.BlockSpec((None, 1, d), lambda bi, m: (layer + 1, 0, 0)),
            pl.BlockSpec((None, None, 3, d), lambda bi, m: (nxt, bi, 0, 0)),
        ],
        out_specs=out_specs,
        out_shape=out_shape,
        scratch_shapes=[pltpu.VMEM((kdim, d), BF16)],
        compiler_params=pltpu.CompilerParams(
            dimension_semantics=("arbitrary", "arbitrary"),
            vmem_limit_bytes=_vmem_limit(est)),
        name="outproj_last" if last else "outproj",
    )(ya, yr, w_out, x, mod, gains, mod)


def kernel(x, c, norm_gain, w_ada, b_ada, w_in, w_out, ret_decay_logit_f, ret_decay_logit_b,
           final_gain):
    b, s, d = x.shape
    depth = w_in.shape[0]
    attn_w = d // 2
    ret_w = d // 2
    n_heads = attn_w // HEAD_DIM_ATTN
    ret_v = ret_w // N_HEADS_RET
    ret_qk = ret_v // 2
    assert ret_qk == LANE and ret_v == 2 * LANE and HEAD_DIM_ATTN == LANE
    assert attn_w == IN_COL_TILE and N_HEADS_RET * ret_qk * 2 == IN_COL_TILE
    assert w_in.shape[2] == 7 * IN_COL_TILE and s % ROW_TILE == 0
    dilations = tuple(dl for _, dl in DILATION_PATTERNS)
    assert all(IN_SUB_ROWS % (dl * 16) == 0 for dl in dilations) and ROW_TILE % IN_SUB_ROWS == 0
    assert dilations[0] == 1 and all(b_ % a_ == 0 for a_, b_ in zip(dilations, dilations[1:]))

    colscale = jnp.ones((7, IN_COL_TILE), F32)
    colscale = colscale.at[0, :].set(HEAD_DIM_ATTN ** -0.5 * LOG2_E)
    colscale = colscale.at[4, N_HEADS_RET * ret_qk:].set(ret_qk ** -0.5)
    colscale = colscale.reshape(1, 7 * IN_COL_TILE)

    c_pad = jnp.zeros((C_PAD, d), F32).at[:b].set(c)
    mod = _adaln(c_pad, w_ada, b_ada).reshape(depth, C_PAD, 3, d)
    gains = jnp.concatenate([norm_gain, final_gain[None]], axis=0).reshape(depth + 1, 1, d)

    def lanes(v):
        return jnp.broadcast_to(v[:, :, None, None, None], (depth, N_HEADS_RET, 1, 8, 2 * LANE))
    dl = jnp.concatenate([lanes(ret_decay_logit_f), lanes(ret_decay_logit_b)], axis=2)

    h = _modulate(x, gains, mod, 0)
    out = None
    for layer in range(depth):
        qkv_a = _inproj_attn(h, w_in, colscale, layer, 3, lambda j: j, dilations)
        qkv_r = _inproj(h, w_in, colscale, layer, 2, lambda j: j + 4, BF16, "inproj_ret")
        z = _inproj(h, w_in, colscale, layer, 2, lambda j: 3 + 3 * j, F32, "inproj_gate")
        ya = _attention(qkv_a, z, n_heads)
        yr = _retention(qkv_r, z, dl[layer], n_heads // 2)
        last = layer == depth - 1
        res = _outproj(ya, yr, w_out, x, mod, gains, layer, last)
        if last:
            out = res
        else:
            x, h = res
    return out
```

```python
import functools

import jax
import jax.numpy as jnp
from jax import lax
from jax.experimental import pallas as pl
from jax.experimental.pallas import tpu as pltpu

F32 = jnp.float32
BF16 = jnp.bfloat16

HEAD_DIM_ATTN = 128
DILATION_PATTERNS = ((128, 1), (512, 4), (2048, 16))
N_HEADS_RET = 4
RET_CHUNK = 128
NORM_EPS = 1e-6
LOG2_E = 1.4426950408889634

LANE = 128
VMEM_CAP_BYTES = 60000 * 1024
TEMP_ALLOWANCE = 8 * 2**20

ROW_TILE = 1024
NORM_ROW_TILE = 512
OUT_ROW_TILE = 256
IN_COL_TILE = 1024
ADA_COL_TILE = 1024
ATTN_TQ = 128
ATTN_UNROLL = 32
RET_UNROLL = 32
IN_SUB_ROWS = 256
OUT_SUB_ROWS = 128
CAST_ROWS = 256
C_PAD = 16


def _vmem_limit(nbytes):
    return int(min(VMEM_CAP_BYTES, nbytes + TEMP_ALLOWANCE))


def _silu(v):
    return v * jax.nn.sigmoid(v)


def _cast_weight(w_ref, wbf_ref):
    n = w_ref.shape[0] // CAST_ROWS

    def body(i, carry):
        rows = pl.ds(pl.multiple_of(i * CAST_ROWS, CAST_ROWS), CAST_ROWS)
        wbf_ref[rows, :] = w_ref[rows, :].astype(BF16)
        return carry

    lax.fori_loop(0, n, body, 0)


def _ada_kernel(c_ref, w_ref, b_ref, o_ref):
    c_act = _silu(c_ref[...])
    acc = jnp.dot(c_act.astype(BF16), w_ref[...].astype(BF16), preferred_element_type=F32)
    o_ref[...] = acc + b_ref[...]


def _adaln(c_pad, w_ada, b_ada):
    depth, d, n3 = w_ada.shape
    tn = ADA_COL_TILE
    est = 2 * d * tn * 4 + d * tn * 2 + 4 * C_PAD * (d + 2 * tn) * 4
    return pl.pallas_call(
        _ada_kernel,
        grid=(depth, n3 // tn),
        in_specs=[
            pl.BlockSpec((C_PAD, d), lambda l, j: (0, 0)),
            pl.BlockSpec((None, d, tn), lambda l, j: (l, 0, j)),
            pl.BlockSpec((None, 1, tn), lambda l, j: (l, 0, j)),
        ],
        out_specs=pl.BlockSpec((None, C_PAD, tn), lambda l, j: (l, 0, j)),
        out_shape=jax.ShapeDtypeStruct((depth, C_PAD, n3), F32),
        compiler_params=pltpu.CompilerParams(
            dimension_semantics=("arbitrary", "arbitrary"),
            vmem_limit_bytes=_vmem_limit(est)),
        name="adaln_mod",
    )(c_pad, w_ada, b_ada.reshape(depth, 1, n3))


def _modulated_norm(x, g, shift, scale):
    y = x * lax.rsqrt(jnp.mean(x * x, axis=-1, keepdims=True) + NORM_EPS)
    return y * (g * (1.0 + scale)) + shift


def _modulate_kernel(x_ref, g_ref, mod_ref, h_ref):
    h = _modulated_norm(x_ref[...], g_ref[...], mod_ref[0:1, :], mod_ref[1:2, :])
    h_ref[...] = h.astype(BF16)


def _modulate(x, gains, mod, layer):
    b, s, d = x.shape
    tm = NORM_ROW_TILE
    est = 2 * tm * d * (4 + 2) + 3 * tm * d * 4
    return pl.pallas_call(
        _modulate_kernel,
        grid=(b, s // tm),
        in_specs=[
            pl.BlockSpec((None, tm, d), lambda bi, m: (bi, m, 0)),
            pl.BlockSpec((None, 1, d), lambda bi, m: (layer, 0, 0)),
            pl.BlockSpec((None, None, 3, d), lambda bi, m: (layer, bi, 0, 0)),
        ],
        out_specs=pl.BlockSpec((None, tm, d), lambda bi, m: (bi, m, 0)),
        out_shape=jax.ShapeDtypeStruct((b, s, d), BF16),
        compiler_params=pltpu.CompilerParams(
            dimension_semantics=("arbitrary", "arbitrary"),
            vmem_limit_bytes=_vmem_limit(est)),
        name="modulate0",
    )(x, gains, mod)


def _project_tile(h_ref, w_ref, cs_ref, wbf_ref):
    @pl.when((pl.program_id(1) == 0) & (pl.program_id(2) == 0))
    def _():
        _cast_weight(w_ref, wbf_ref)

    acc = jnp.dot(h_ref[...], wbf_ref[...], preferred_element_type=F32)
    return acc * cs_ref[...]


def _inproj_kernel(h_ref, w_ref, cs_ref, o_ref, wbf_ref):
    acc = _project_tile(h_ref, w_ref, cs_ref, wbf_ref)
    for j in range(o_ref.shape[0]):
        o_ref[j] = acc[:, j * LANE:(j + 1) * LANE].astype(o_ref.dtype)


def _inproj_attn_kernel(h_ref, w_ref, cs_ref, *rest, dilations):
    nd = len(dilations)
    out_refs = rest[:nd]
    wbf_ref = rest[nd]
    stage_refs = rest[nd + 1:]

    tile = pl.program_id(2)

    @pl.when((pl.program_id(0) == 0) & (pl.program_id(1) == 0))
    def _():
        _cast_weight(w_ref, wbf_ref.at[tile])

    nb, sub = stage_refs[0].shape[0], stage_refs[0].shape[2]
    for t in range(len(stage_refs) // (nd - 1)):
        stages = stage_refs[t * (nd - 1):(t + 1) * (nd - 1)]
        s0 = t * sub
        acc = jnp.dot(h_ref[s0:s0 + sub, :], wbf_ref[tile], preferred_element_type=F32)
        acc = acc * cs_ref[...]
        for j in range(nb):
            blk = acc[:, j * LANE:(j + 1) * LANE]
            stages[0][j, 0] = blk
            out_refs[0][j, 0, s0:s0 + sub, :] = blk.astype(out_refs[0].dtype)
        for k in range(1, nd):
            prev, dil = dilations[k - 1], dilations[k]
            q, n = dil // prev, sub // dil
            for j in range(nb):
                for r in range(dil):
                    ra, rb = r % prev, r // prev
                    v = stages[k - 1][j, ra, pl.ds(rb, n, stride=q), :]
                    out_refs[k][j, r, s0 // dil:s0 // dil + n, :] = v.astype(out_refs[k].dtype)
                    if k < nd - 1:
                        stages[k][j, r] = v


def _inproj_specs(h, layer, tile_of):
    b, s, d = h.shape
    tm, tn = ROW_TILE, IN_COL_TILE
    return [
        pl.BlockSpec((None, tm, d), lambda j, bi, m: (bi, m, 0)),
        pl.BlockSpec((None, d, tn), lambda j, bi, m: (layer, 0, tile_of(j))),
        pl.BlockSpec((1, tn), lambda j, bi, m: (0, tile_of(j))),
    ]


def _inproj(h, w_in, colscale, layer, n_tiles, tile_of, out_dtype, name):
    b, s, d = h.shape
    tm, tn = ROW_TILE, IN_COL_TILE
    nb = tn // LANE
    osz = jnp.dtype(out_dtype).itemsize
    est = 2 * d * tn * 4 + d * tn * 2 + 2 * tm * d * 2 + 2 * tm * tn * osz + 2 * tm * tn * 4
    return pl.pallas_call(
        _inproj_kernel,
        grid=(n_tiles, b, s // tm),
        in_specs=_inproj_specs(h, layer, tile_of),
        out_specs=pl.BlockSpec((None, nb, tm, LANE), lambda j, bi, m: (bi, j, m, 0)),
        out_shape=jax.ShapeDtypeStruct((b, n_tiles * nb, s, LANE), out_dtype),
        scratch_shapes=[pltpu.VMEM((d, tn), BF16)],
        compiler_params=pltpu.CompilerParams(
            dimension_semantics=("arbitrary", "arbitrary", "arbitrary"),
            vmem_limit_bytes=_vmem_limit(est)),
        name=name,
    )(h, w_in, colscale)


def _inproj_attn(h, w_in, colscale, layer, n_tiles, tile_of, dilations):
    b, s, d = h.shape
    tm, tn = ROW_TILE, IN_COL_TILE
    nb = tn // LANE
    nd = len(dilations)
    est = (d * tn * 4 + n_tiles * d * tn * 2 + 2 * tm * d * 2 + nd * 2 * tm * tn * 2
           + 3 * tm * tn * 4)

    def wtile(bi, m, j):
        return jnp.where((bi == 0) & (m == 0), tile_of(j), tile_of(n_tiles - 1))

    return pl.pallas_call(
        functools.partial(_inproj_attn_kernel, dilations=dilations),
        grid=(b, s // tm, n_tiles),
        in_specs=[
            pl.BlockSpec((None, tm, d), lambda bi, m, j: (bi, m, 0)),
            pl.BlockSpec((None, d, tn), lambda bi, m, j: (layer, 0, wtile(bi, m, j)),
                         pipeline_mode=pl.Buffered(1)),
            pl.BlockSpec((1, tn), lambda bi, m, j: (0, tile_of(j))),
        ],
        out_specs=[pl.BlockSpec((None, nb, dl, tm // dl, LANE),
                                lambda bi, m, j: (bi, j, 0, m, 0)) for dl in dilations],
        out_shape=[jax.ShapeDtypeStruct((b, n_tiles * nb, dl, s // dl, LANE), BF16)
                   for dl in dilations],
        scratch_shapes=[pltpu.VMEM((n_tiles, d, tn), BF16)]
        + [pltpu.VMEM((nb, dl, IN_SUB_ROWS // dl, LANE), F32)
           for dl in dilations[:-1]] * (tm // IN_SUB_ROWS),
        compiler_params=pltpu.CompilerParams(
            dimension_semantics=("arbitrary", "arbitrary", "arbitrary"),
            vmem_limit_bytes=_vmem_limit(est)),
        name="inproj_attn",
    )(h, w_in, colscale)


def _attn_kernel(*refs, n_heads, patterns, seq):
    npat = len(patterns)
    qkv_refs = refs[:3 * npat]
    z_ref, o_ref, acc_sc, m_sc, den_sc, bias_sc = refs[3 * npat:]
    tq = ATTN_TQ
    radius = patterns[0][0] // (2 * patterns[0][1])
    tk = tq + 2 * radius

    head = pl.program_id(1)
    expo = (head + 1).astype(F32) * (-8.0 / n_heads)
    slope = jnp.exp2(jnp.full((tq, tk), 1.0, F32) * expo)
    row = lax.broadcasted_iota(jnp.int32, (tq, tk), 0)
    col = lax.broadcasted_iota(jnp.int32, (tq, tk), 1)
    for p, (_, dil) in enumerate(patterns):
        for case in range(3):
            rel = jnp.abs(col - row - case * radius)
            bias = -slope * (rel * dil).astype(F32) * LOG2_E
            bias_sc[p * 3 + case] = jnp.where(rel <= radius, bias, -jnp.inf)

    nblocks = seq // tq
    strided = [p for p in range(npat) if patterns[p][1] != 1]
    slot_of = {p: i for i, p in enumerate(strided)}
    for p in strided + [p for p in range(npat) if patterns[p][1] == 1]:
        dil = patterns[p][1]
        sub_len = seq // dil
        nblk = sub_len // tq
        shift = nblk.bit_length() - 1
        q_ref, k_ref, v_ref = qkv_refs[3 * p:3 * p + 3]

        def block(g, carry, p=p, dil=dil, sub_len=sub_len, nblk=nblk, shift=shift,
                  q_ref=q_ref, k_ref=k_ref, v_ref=v_ref):
            r = lax.shift_right_logical(g, shift)
            i = lax.bitwise_and(g, nblk - 1)
            l0 = pl.multiple_of(i * tq, tq)
            ks = pl.multiple_of(jnp.clip(l0 - radius, 0, sub_len - tk), radius)
            case = (l0 - ks) // radius
            q = q_ref[r, pl.ds(l0, tq), :]
            k = k_ref[r, pl.ds(ks, tk), :]
            v = v_ref[r, pl.ds(ks, tk), :]
            s = lax.dot_general(q, k, (((1,), (1,)), ((), ())), preferred_element_type=F32)
            s = s + bias_sc[p * 3 + case]
            m = jnp.max(s, axis=-1, keepdims=True)
            pe = jnp.exp2(s - m)
            den = jnp.sum(pe, axis=-1, keepdims=True)
            acc = jnp.dot(pe.astype(BF16), v, preferred_element_type=F32)
            if dil != 1:
                rows = pl.ds(l0 * dil + r, tq, stride=dil)
                acc_sc[slot_of[p], rows, :] = acc
                m_sc[slot_of[p], rows, :] = jnp.broadcast_to(m, (tq, LANE))
                den_sc[slot_of[p], rows, :] = jnp.broadcast_to(den, (tq, LANE))
                return carry
            rows = pl.ds(l0, tq)
            others = [m_sc[i, rows, :] for i in range(len(strided))]
            mx = jnp.maximum(functools.reduce(jnp.maximum, others), m)
            e = jnp.exp2(m - mx)
            num = e * acc
            wsum = e * den
            for i, ms in enumerate(others):
                e = jnp.exp2(ms - mx)
                num = num + e * acc_sc[i, rows, :]
                wsum = wsum + e * den_sc[i, rows, :]
            o_ref[rows, :] = ((num / wsum) * _silu(z_ref[rows, :])).astype(o_ref.dtype)
            return carry

        lax.fori_loop(0, nblocks, block, 0, unroll=ATTN_UNROLL)


def _attention(qkv_views, z, n_heads):
    patterns = DILATION_PATTERNS
    b, _, _, s, _ = qkv_views[0].shape
    radius = patterns[0][0] // (2 * patterns[0][1])
    assert all(w // (2 * dl) == radius for w, dl in patterns)
    tk = ATTN_TQ + 2 * radius
    for _, dl in patterns:
        nblk = (s // dl) // ATTN_TQ
        assert (s // dl) % ATTN_TQ == 0 and s // dl >= tk and nblk & (nblk - 1) == 0
    assert (s // ATTN_TQ) % ATTN_UNROLL == 0
    npat = len(patterns)

    operands, in_specs = [], []
    for (_, dl), view in zip(patterns, qkv_views):
        for t in range(3):
            operands.append(view)
            in_specs.append(pl.BlockSpec(
                (None, None, dl, s // dl, LANE),
                lambda bi, h, t=t: (bi, t * n_heads + h, 0, 0, 0)))
    operands.append(z)
    in_specs.append(pl.BlockSpec((None, None, s, LANE), lambda bi, h: (bi, h, 0, 0)))

    assert sum(dl == 1 for _, dl in patterns) == 1 and npat > 1
    est = (2 * 3 * npat * s * LANE * 2 + 2 * s * LANE * 4 + 2 * s * LANE * 2
           + 3 * (npat - 1) * s * LANE * 4 + 3 * npat * ATTN_TQ * tk * 4)
    return pl.pallas_call(
        functools.partial(_attn_kernel, n_heads=n_heads, patterns=patterns, seq=s),
        grid=(b, n_heads),
        in_specs=in_specs,
        out_specs=pl.BlockSpec((None, None, s, LANE), lambda bi, h: (bi, h, 0, 0)),
        out_shape=jax.ShapeDtypeStruct((b, n_heads, s, LANE), BF16),
        scratch_shapes=[
            pltpu.VMEM((npat - 1, s, LANE), F32),
            pltpu.VMEM((npat - 1, s, LANE), F32),
            pltpu.VMEM((npat - 1, s, LANE), F32),
            pltpu.VMEM((3 * npat, ATTN_TQ, tk), F32),
        ],
        compiler_params=pltpu.CompilerParams(
            dimension_semantics=("arbitrary", "arbitrary"),
            vmem_limit_bytes=_vmem_limit(est)),
        name="dilated_attention",
    )(*operands)


def _log_sigmoid(v):
    return jnp.minimum(v, 0.0) - jnp.log1p(jnp.exp(-jnp.abs(v)))


def _ret_kernel(q_ref, k_ref, v_ref, z_ref, dl_ref, o_ref, oi_sc, uf_sc, ub_sc, st_sc, *, seq):
    c = RET_CHUNK
    n = seq // c
    dk = q_ref.shape[-1]
    dv = 2 * LANE

    lg_f = _log_sigmoid(dl_ref[0, 0:1, :])
    lg_b = _log_sigmoid(dl_ref[1, 0:1, :])
    ri = lax.broadcasted_iota(jnp.int32, (c, c), 0)
    ci = lax.broadcasted_iota(jnp.int32, (c, c), 1)
    rel = (ri - ci).astype(F32)
    dec = (jnp.where(rel >= 0, jnp.exp(jnp.maximum(rel, 0.0) * lg_f[:, :c]), 0.0)
           + jnp.where(rel <= 0, jnp.exp(jnp.maximum(-rel, 0.0) * lg_b[:, :c]), 0.0))
    idx_v = lax.broadcasted_iota(jnp.int32, (c, dv), 0).astype(F32)
    idx_k = lax.broadcasted_iota(jnp.int32, (c, dk), 0).astype(F32)
    xi_f = jnp.exp((idx_v + 1.0) * lg_f)
    xi_b = jnp.exp((c - idx_v) * lg_b)
    zeta_f = jnp.exp((c - 1.0 - idx_k) * lg_f[:, :dk])
    zeta_b = jnp.exp(idx_k * lg_b[:, :dk])
    g_f = jnp.exp(c * lg_f)
    g_b = jnp.exp(c * lg_b)

    def intra(i, carry):
        rows = pl.ds(pl.multiple_of(i * c, c), c)
        q = q_ref[rows, :]
        k = k_ref[rows, :]
        v = jnp.concatenate([v_ref[0, rows, :], v_ref[1, rows, :]], axis=-1)
        qk = lax.dot_general(q, k, (((1,), (1,)), ((), ())), preferred_element_type=F32)
        kf = k.astype(F32)
        lhs = jnp.concatenate([(qk * dec).astype(BF16),
                               (kf * zeta_f).T.astype(BF16),
                               (kf * zeta_b).T.astype(BF16)], axis=0)
        res = jnp.dot(lhs, v, preferred_element_type=F32)
        oi_sc[rows, :] = res[:c]
        uf_sc[i] = res[c:c + dk]
        ub_sc[i] = res[c + dk:]
        return carry

    lax.fori_loop(0, n, intra, 0, unroll=RET_UNROLL)

    def rec_f(i, st):
        st_sc[i, :, 0:dv] = st.astype(BF16)
        return st * g_f + uf_sc[i]

    def rec_b(t, st):
        i = n - 1 - t
        st_sc[i, :, dv:2 * dv] = st.astype(BF16)
        return st * g_b + ub_sc[i]

    lax.fori_loop(0, n, rec_f, jnp.zeros((dk, dv), F32))
    lax.fori_loop(0, n, rec_b, jnp.zeros((dk, dv), F32))

    def cross(i, carry):
        rows = pl.ds(pl.multiple_of(i * c, c), c)
        cr = jnp.dot(q_ref[rows, :], st_sc[i], preferred_element_type=F32)
        o = oi_sc[rows, :] + cr[:, :dv] * xi_f + cr[:, dv:] * xi_b
        y = o * lax.rsqrt(jnp.mean(o * o, axis=-1, keepdims=True) + NORM_EPS)
        z = jnp.concatenate([z_ref[0, rows, :], z_ref[1, rows, :]], axis=-1)
        out = (y * _silu(z)).astype(o_ref.dtype)
        o_ref[0, rows, :] = out[:, :LANE]
        o_ref[1, rows, :] = out[:, LANE:]
        return carry

    lax.fori_loop(0, n, cross, 0, unroll=RET_UNROLL)


def _retention(qkv, z, dl, z_blk2):
    b, _, s, _ = qkv.shape
    nh = N_HEADS_RET
    dv = 2 * LANE
    n = s // RET_CHUNK
    est = (2 * 2 * s * LANE * 2 + 2 * s * dv * 2 + 2 * s * dv * 4 + 2 * s * dv * 2
           + s * dv * 4 + 2 * n * LANE * dv * 4 + n * LANE * 2 * dv * 2)
    return pl.pallas_call(
        functools.partial(_ret_kernel, seq=s),
        grid=(b, nh),
        in_specs=[
            pl.BlockSpec((None, None, s, LANE), lambda bi, h: (bi, h, 0, 0)),
            pl.BlockSpec((None, None, s, LANE), lambda bi, h: (bi, nh + h, 0, 0)),
            pl.BlockSpec((None, 2, s, LANE), lambda bi, h: (bi, nh + h, 0, 0)),
            pl.BlockSpec((None, 2, s, LANE), lambda bi, h: (bi, z_blk2 + h, 0, 0)),
            pl.BlockSpec((None, 2, 8, dv), lambda bi, h: (h, 0, 0, 0)),
        ],
        out_specs=pl.BlockSpec((None, 2, s, LANE), lambda bi, h: (bi, h, 0, 0)),
        out_shape=jax.ShapeDtypeStruct((b, 2 * nh, s, LANE), BF16),
        scratch_shapes=[
            pltpu.VMEM((s, dv), F32),
            pltpu.VMEM((n, LANE, dv), F32),
            pltpu.VMEM((n, LANE, dv), F32),
            pltpu.VMEM((n, LANE, 2 * dv), BF16),
        ],
        compiler_params=pltpu.CompilerParams(
            dimension_semantics=("arbitrary", "arbitrary"),
            vmem_limit_bytes=_vmem_limit(est)),
        name="retention",
    )(qkv, qkv, qkv, z, dl)


def _outproj_kernel(ya_ref, yr_ref, w_ref, x_ref, mod_ref, gn_ref, modn_ref, *rest, last):
    if last:
        o_ref, wbf_ref = rest
    else:
        xo_ref, h_ref, wbf_ref = rest

    @pl.when((pl.program_id(0) == 0) & (pl.program_id(1) == 0))
    def _():
        _cast_weight(w_ref, wbf_ref)

    tm = x_ref.shape[0]
    for s0 in range(0, tm, OUT_SUB_ROWS):
        rows = slice(s0, s0 + OUT_SUB_ROWS)
        y = jnp.concatenate([ya_ref[j, rows, :] for j in range(ya_ref.shape[0])]
                            + [yr_ref[j, rows, :] for j in range(yr_ref.shape[0])], axis=-1)
        out = jnp.dot(y, wbf_ref[...], preferred_element_type=F32)
        xn = x_ref[rows, :] + mod_ref[2:3, :] * out
        if last:
            o_ref[rows, :] = (xn * lax.rsqrt(jnp.mean(xn * xn, axis=-1, keepdims=True) + NORM_EPS)
                              * gn_ref[...])
        else:
            xo_ref[rows, :] = xn
            h = _modulated_norm(xn, gn_ref[...], modn_ref[0:1, :], modn_ref[1:2, :])
            h_ref[rows, :] = h.astype(BF16)


def _outproj(ya, yr, w_out, x, mod, gains, layer, last):
    b, s, d = x.shape
    tm = OUT_ROW_TILE
    nba, nbr = ya.shape[1], yr.shape[1]
    kdim = (nba + nbr) * LANE
    nxt = layer if last else layer + 1
    est = (kdim * d * 4 + kdim * d * 2 + 2 * tm * kdim * 2 + 2 * tm * d * 4
           + 2 * tm * d * 4 + 2 * tm * d * 2 + 3 * tm * d * 4)
    if last:
        out_shape = jax.ShapeDtypeStruct((b, s, d), F32)
        out_specs = pl.BlockSpec((None, tm, d), lambda bi, m: (bi, m, 0))
    else:
        out_shape = (jax.ShapeDtypeStruct((b, s, d), F32), jax.ShapeDtypeStruct((b, s, d), BF16))
        out_specs = (pl.BlockSpec((None, tm, d), lambda bi, m: (bi, m, 0)),
                     pl.BlockSpec((None, tm, d), lambda bi, m: (bi, m, 0)))
    return pl.pallas_call(
        functools.partial(_outproj_kernel, last=last),
        grid=(b, s // tm),
        in_specs=[
            pl.BlockSpec((None, nba, tm, LANE), lambda bi, m: (bi, 0, m, 0)),
            pl.BlockSpec((None, nbr, tm, LANE), lambda bi, m: (bi, 0, m, 0)),
            pl.BlockSpec((None, kdim, d), lambda bi, m: (layer, 0, 0),
                         pipeline_mode=pl.Buffered(1)),
            pl.BlockSpec((None, tm, d), lambda bi, m: (bi, m, 0)),
            pl.BlockSpec((None, None, 3, d), lambda bi, m: (layer, bi, 0, 0)),
            pl.BlockSpec((None, 1, d), lambda bi, m: (layer + 1, 0, 0)),
            pl.BlockSpec((None, None, 3, d), lambda bi, m: (nxt, bi, 0, 0)),
        ],
        out_specs=out_specs,
        out_shape=out_shape,
        scratch_shapes=[pltpu.VMEM((kdim, d), BF16)],
        compiler_params=pltpu.CompilerParams(
            dimension_semantics=("arbitrary", "arbitrary"),
            vmem_limit_bytes=_vmem_limit(est)),
        name="outproj_last" if last else "outproj",
    )(ya, yr, w_out, x, mod, gains, mod)


def kernel(x, c, norm_gain, w_ada, b_ada, w_in, w_out, ret_decay_logit_f, ret_decay_logit_b,
           final_gain):
    b, s, d = x.shape
    depth = w_in.shape[0]
    attn_w = d // 2
    ret_w = d // 2
    n_heads = attn_w // HEAD_DIM_ATTN
    ret_v = ret_w // N_HEADS_RET
    ret_qk = ret_v // 2
    assert ret_qk == LANE and ret_v == 2 * LANE and HEAD_DIM_ATTN == LANE
    assert attn_w == IN_COL_TILE and N_HEADS_RET * ret_qk * 2 == IN_COL_TILE
    assert w_in.shape[2] == 7 * IN_COL_TILE and s % ROW_TILE == 0
    dilations = tuple(dl for _, dl in DILATION_PATTERNS)
    assert all(IN_SUB_ROWS % (dl * 16) == 0 for dl in dilations) and ROW_TILE % IN_SUB_ROWS == 0
    assert dilations[0] == 1 and all(b_ % a_ == 0 for a_, b_ in zip(dilations, dilations[1:]))

    colscale = jnp.ones((7, IN_COL_TILE), F32)
    colscale = colscale.at[0, :].set(HEAD_DIM_ATTN ** -0.5 * LOG2_E)
    colscale = colscale.at[4, N_HEADS_RET * ret_qk:].set(ret_qk ** -0.5)
    colscale = colscale.reshape(1, 7 * IN_COL_TILE)

    c_pad = jnp.zeros((C_PAD, d), F32).at[:b].set(c)
    mod = _adaln(c_pad, w_ada, b_ada).reshape(depth, C_PAD, 3, d)
    gains = jnp.concatenate([norm_gain, final_gain[None]], axis=0).reshape(depth + 1, 1, d)

    def lanes(v):
        return jnp.broadcast_to(v[:, :, None, None, None], (depth, N_HEADS_RET, 1, 8, 2 * LANE))
    dl = jnp.concatenate([lanes(ret_decay_logit_f), lanes(ret_decay_logit_b)], axis=2)

    h = _modulate(x, gains, mod, 0)
    out = None
    for layer in range(depth):
        qkv_a = _inproj_attn(h, w_in, colscale, layer, 3, lambda j: j, dilations)
        qkv_r = _inproj(h, w_in, colscale, layer, 2, lambda j: j + 4, BF16, "inproj_ret")
        z = _inproj(h, w_in, colscale, layer, 2, lambda j: 3 + 3 * j, F32, "inproj_gate")
        ya = _attention(qkv_a, z, n_heads)
        yr = _retention(qkv_r, z, dl[layer], n_heads // 2)
        last = layer == depth - 1
        res = _outproj(ya, yr, w_out, x, mod, gains, layer, last)
        if last:
            out = res
        else:
            x, h = res
    return out
```

```python
import functools

import jax
import jax.numpy as jnp
from jax import lax
from jax.experimental import pallas as pl
from jax.experimental.pallas import tpu as pltpu

F32 = jnp.float32
BF16 = jnp.bfloat16

HEAD_DIM_ATTN = 128
DILATION_PATTERNS = ((128, 1), (512, 4), (2048, 16))
N_HEADS_RET = 4
RET_CHUNK = 128
NORM_EPS = 1e-6
LOG2_E = 1.4426950408889634

LANE = 128
VMEM_CAP_BYTES = 60000 * 1024
TEMP_ALLOWANCE = 8 * 2**20

ROW_TILE = 1024
NORM_ROW_TILE = 512
OUT_ROW_TILE = 256
IN_COL_TILE = 1024
ADA_COL_TILE = 1024
ATTN_TQ = 128
ATTN_UNROLL = 32
RET_UNROLL = 32
IN_SUB_ROWS = 256
OUT_SUB_ROWS = 128
CAST_ROWS = 256
C_PAD = 16


def _vmem_limit(nbytes):
    return int(min(VMEM_CAP_BYTES, nbytes + TEMP_ALLOWANCE))


def _silu(v):
    return v * jax.nn.sigmoid(v)


def _cast_weight(w_ref, wbf_ref):
    n = w_ref.shape[0] // CAST_ROWS

    def body(i, carry):
        rows = pl.ds(pl.multiple_of(i * CAST_ROWS, CAST_ROWS), CAST_ROWS)
        wbf_ref[rows, :] = w_ref[rows, :].astype(BF16)
        return carry

    lax.fori_loop(0, n, body, 0)


def _ada_kernel(c_ref, w_ref, b_ref, o_ref):
    c_act = _silu(c_ref[...])
    acc = jnp.dot(c_act.astype(BF16), w_ref[...].astype(BF16), preferred_element_type=F32)
    o_ref[...] = acc + b_ref[...]


def _adaln(c_pad, w_ada, b_ada):
    depth, d, n3 = w_ada.shape
    tn = ADA_COL_TILE
    est = 2 * d * tn * 4 + d * tn * 2 + 4 * C_PAD * (d + 2 * tn) * 4
    return pl.pallas_call(
        _ada_kernel,
        grid=(depth, n3 // tn),
        in_specs=[
            pl.BlockSpec((C_PAD, d), lambda l, j: (0, 0)),
            pl.BlockSpec((None, d, tn), lambda l, j: (l, 0, j)),
            pl.BlockSpec((None, 1, tn), lambda l, j: (l, 0, j)),
        ],
        out_specs=pl.BlockSpec((None, C_PAD, tn), lambda l, j: (l, 0, j)),
        out_shape=jax.ShapeDtypeStruct((depth, C_PAD, n3), F32),
        compiler_params=pltpu.CompilerParams(
            dimension_semantics=("arbitrary", "arbitrary"),
            vmem_limit_bytes=_vmem_limit(est)),
        name="adaln_mod",
    )(c_pad, w_ada, b_ada.reshape(depth, 1, n3))


def _modulated_norm(x, g, shift, scale):
    y = x * lax.rsqrt(jnp.mean(x * x, axis=-1, keepdims=True) + NORM_EPS)
    return y * (g * (1.0 + scale)) + shift


def _modulate_kernel(x_ref, g_ref, mod_ref, h_ref):
    h = _modulated_norm(x_ref[...], g_ref[...], mod_ref[0:1, :], mod_ref[1:2, :])
    h_ref[...] = h.astype(BF16)


def _modulate(x, gains, mod, layer):
    b, s, d = x.shape
    tm = NORM_ROW_TILE
    est = 2 * tm * d * (4 + 2) + 3 * tm * d * 4
    return pl.pallas_call(
        _modulate_kernel,
        grid=(b, s // tm),
        in_specs=[
            pl.BlockSpec((None, tm, d), lambda bi, m: (bi, m, 0)),
            pl.BlockSpec((None, 1, d), lambda bi, m: (layer, 0, 0)),
            pl.BlockSpec((None, None, 3, d), lambda bi, m: (layer, bi, 0, 0)),
        ],
        out_specs=pl.BlockSpec((None, tm, d), lambda bi, m: (bi, m, 0)),
        out_shape=jax.ShapeDtypeStruct((b, s, d), BF16),
        compiler_params=pltpu.CompilerParams(
            dimension_semantics=("arbitrary", "arbitrary"),
            vmem_limit_bytes=_vmem_limit(est)),
        name="modulate0",
    )(x, gains, mod)


def _project_tile(h_ref, w_ref, cs_ref, wbf_ref):
    @pl.when((pl.program_id(1) == 0) & (pl.program_id(2) == 0))
    def _():
        _cast_weight(w_ref, wbf_ref)

    acc = jnp.dot(h_ref[...], wbf_ref[...], preferred_element_type=F32)
    return acc * cs_ref[...]


def _inproj_kernel(h_ref, w_ref, cs_ref, o_ref, wbf_ref):
    acc = _project_tile(h_ref, w_ref, cs_ref, wbf_ref)
    for j in range(o_ref.shape[0]):
        o_ref[j] = acc[:, j * LANE:(j + 1) * LANE].astype(o_ref.dtype)


def _inproj_attn_kernel(h_ref, w_ref, cs_ref, *rest, dilations):
    nd = len(dilations)
    out_refs = rest[:nd]
    wbf_ref = rest[nd]
    stage_refs = rest[nd + 1:]

    @pl.when((pl.program_id(1) == 0) & (pl.program_id(2) == 0))
    def _():
        _cast_weight(w_ref, wbf_ref)

    nb, sub = stage_refs[0].shape[0], stage_refs[0].shape[2]
    for t in range(len(stage_refs) // (nd - 1)):
        stages = stage_refs[t * (nd - 1):(t + 1) * (nd - 1)]
        s0 = t * sub
        acc = jnp.dot(h_ref[s0:s0 + sub, :], wbf_ref[...], preferred_element_type=F32)
        acc = acc * cs_ref[...]
        for j in range(nb):
            blk = acc[:, j * LANE:(j + 1) * LANE]
            stages[0][j, 0] = blk
            out_refs[0][j, 0, s0:s0 + sub, :] = blk.astype(out_refs[0].dtype)
        for k in range(1, nd):
            prev, dil = dilations[k - 1], dilations[k]
            q, n = dil // prev, sub // dil
            for j in range(nb):
                for r in range(dil):
                    ra, rb = r % prev, r // prev
                    v = stages[k - 1][j, ra, pl.ds(rb, n, stride=q), :]
                    out_refs[k][j, r, s0 // dil:s0 // dil + n, :] = v.astype(out_refs[k].dtype)
                    if k < nd - 1:
                        stages[k][j, r] = v


def _inproj_specs(h, layer, tile_of):
    b, s, d = h.shape
    tm, tn = ROW_TILE, IN_COL_TILE
    return [
        pl.BlockSpec((None, tm, d), lambda j, bi, m: (bi, m, 0)),
        pl.BlockSpec((None, d, tn), lambda j, bi, m: (layer, 0, tile_of(j))),
        pl.BlockSpec((1, tn), lambda j, bi, m: (0, tile_of(j))),
    ]


def _inproj(h, w_in, colscale, layer, n_tiles, tile_of, out_dtype, name):
    b, s, d = h.shape
    tm, tn = ROW_TILE, IN_COL_TILE
    nb = tn // LANE
    osz = jnp.dtype(out_dtype).itemsize
    est = 2 * d * tn * 4 + d * tn * 2 + 2 * tm * d * 2 + 2 * tm * tn * osz + 2 * tm * tn * 4
    return pl.pallas_call(
        _inproj_kernel,
        grid=(n_tiles, b, s // tm),
        in_specs=_inproj_specs(h, layer, tile_of),
        out_specs=pl.BlockSpec((None, nb, tm, LANE), lambda j, bi, m: (bi, j, m, 0)),
        out_shape=jax.ShapeDtypeStruct((b, n_tiles * nb, s, LANE), out_dtype),
        scratch_shapes=[pltpu.VMEM((d, tn), BF16)],
        compiler_params=pltpu.CompilerParams(
            dimension_semantics=("arbitrary", "arbitrary", "arbitrary"),
            vmem_limit_bytes=_vmem_limit(est)),
        name=name,
    )(h, w_in, colscale)


def _inproj_attn(h, w_in, colscale, layer, n_tiles, tile_of, dilations):
    b, s, d = h.shape
    tm, tn = ROW_TILE, IN_COL_TILE
    nb = tn // LANE
    nd = len(dilations)
    est = (2 * d * tn * 4 + d * tn * 2 + 2 * tm * d * 2 + nd * 2 * tm * tn * 2
           + 3 * tm * tn * 4)
    return pl.pallas_call(
        functools.partial(_inproj_attn_kernel, dilations=dilations),
        grid=(n_tiles, b, s // tm),
        in_specs=_inproj_specs(h, layer, tile_of),
        out_specs=[pl.BlockSpec((None, nb, dl, tm // dl, LANE),
                                lambda j, bi, m: (bi, j, 0, m, 0)) for dl in dilations],
        out_shape=[jax.ShapeDtypeStruct((b, n_tiles * nb, dl, s // dl, LANE), BF16)
                   for dl in dilations],
        scratch_shapes=[pltpu.VMEM((d, tn), BF16)]
        + [pltpu.VMEM((nb, dl, IN_SUB_ROWS // dl, LANE), F32)
           for dl in dilations[:-1]] * (tm // IN_SUB_ROWS),
        compiler_params=pltpu.CompilerParams(
            dimension_semantics=("arbitrary", "arbitrary", "arbitrary"),
            vmem_limit_bytes=_vmem_limit(est)),
        name="inproj_attn",
    )(h, w_in, colscale)


def _park_pitch(dil):
    return dil + 8 if dil % 16 == 0 else dil


def _attn_kernel(*refs, n_heads, patterns, seq):
    npat = len(patterns)
    qkv_refs = refs[:3 * npat]
    z_ref, o_ref = refs[3 * npat:3 * npat + 2]
    bias_sc = refs[-1]
    park = refs[3 * npat + 2:-1]
    tq = ATTN_TQ
    radius = patterns[0][0] // (2 * patterns[0][1])
    tk = tq + 2 * radius

    head = pl.program_id(1)
    expo = (head + 1).astype(F32) * (-8.0 / n_heads)
    slope = jnp.exp2(jnp.full((tq, tk), 1.0, F32) * expo)
    row = lax.broadcasted_iota(jnp.int32, (tq, tk), 0)
    col = lax.broadcasted_iota(jnp.int32, (tq, tk), 1)
    for p, (_, dil) in enumerate(patterns):
        for case in range(3):
            rel = jnp.abs(col - row - case * radius)
            bias = -slope * (rel * dil).astype(F32) * LOG2_E
            bias_sc[p * 3 + case] = jnp.where(rel <= radius, bias, -jnp.inf)

    nblocks = seq // tq
    strided = [p for p in range(npat) if patterns[p][1] != 1]
    slot_of = {p: i for i, p in enumerate(strided)}
    for p in strided + [p for p in range(npat) if patterns[p][1] == 1]:
        dil = patterns[p][1]
        sub_len = seq // dil
        nblk = sub_len // tq
        shift = nblk.bit_length() - 1
        q_ref, k_ref, v_ref = qkv_refs[3 * p:3 * p + 3]

        def block(g, carry, p=p, dil=dil, sub_len=sub_len, nblk=nblk, shift=shift,
                  q_ref=q_ref, k_ref=k_ref, v_ref=v_ref):
            r = lax.shift_right_logical(g, shift)
            i = lax.bitwise_and(g, nblk - 1)
            l0 = pl.multiple_of(i * tq, tq)
            ks = pl.multiple_of(jnp.clip(l0 - radius, 0, sub_len - tk), radius)
            case = (l0 - ks) // radius
            q = q_ref[r, pl.ds(l0, tq), :]
            k = k_ref[r, pl.ds(ks, tk), :]
            v = v_ref[r, pl.ds(ks, tk), :]
            s = lax.dot_general(q, k, (((1,), (1,)), ((), ())), preferred_element_type=F32)
            s = s + bias_sc[p * 3 + case]
            m = jnp.max(s, axis=-1, keepdims=True)
            pe = jnp.exp2(s - m)
            den = jnp.sum(pe, axis=-1, keepdims=True)
            acc = jnp.dot(pe.astype(BF16), v, preferred_element_type=F32)
            if dil != 1:
                pitch = _park_pitch(dil)
                rows = pl.ds(l0 * pitch + r, tq, stride=pitch)
                acc_ref, m_ref, den_ref = park[3 * slot_of[p]:3 * slot_of[p] + 3]
                acc_ref[rows, :] = acc
                m_ref[rows, :] = jnp.broadcast_to(m, (tq, LANE))
                den_ref[rows, :] = jnp.broadcast_to(den, (tq, LANE))
                return carry

            def parked(ref, dl):
                pitch = _park_pitch(dl)
                if pitch == dl:
                    return ref[pl.ds(l0, tq), :]
                first = l0 // dl
                return jnp.concatenate(
                    [ref[pl.ds(pl.multiple_of((first + g) * pitch, 8), dl), :]
                     for g in range(tq // dl)], axis=0)

            dils = [patterns[i][1] for i in strided]
            others = [parked(park[3 * i + 1], dl) for i, dl in enumerate(dils)]
            mx = jnp.maximum(functools.reduce(jnp.maximum, others), m)
            e = jnp.exp2(m - mx)
            num = e * acc
            wsum = e * den
            for i, (ms, dl) in enumerate(zip(others, dils)):
                e = jnp.exp2(ms - mx)
                num = num + e * parked(park[3 * i], dl)
                wsum = wsum + e * parked(park[3 * i + 2], dl)
            rows = pl.ds(l0, tq)
            o_ref[rows, :] = ((num / wsum) * _silu(z_ref[rows, :])).astype(o_ref.dtype)
            return carry

        lax.fori_loop(0, nblocks, block, 0, unroll=ATTN_UNROLL)


def _attention(qkv_views, z, n_heads):
    patterns = DILATION_PATTERNS
    b, _, _, s, _ = qkv_views[0].shape
    radius = patterns[0][0] // (2 * patterns[0][1])
    assert all(w // (2 * dl) == radius for w, dl in patterns)
    tk = ATTN_TQ + 2 * radius
    for _, dl in patterns:
        nblk = (s // dl) // ATTN_TQ
        assert (s // dl) % ATTN_TQ == 0 and s // dl >= tk and nblk & (nblk - 1) == 0
    assert (s // ATTN_TQ) % ATTN_UNROLL == 0
    npat = len(patterns)

    operands, in_specs = [], []
    for (_, dl), view in zip(patterns, qkv_views):
        for t in range(3):
            operands.append(view)
            in_specs.append(pl.BlockSpec(
                (None, None, dl, s // dl, LANE),
                lambda bi, h, t=t: (bi, t * n_heads + h, 0, 0, 0)))
    operands.append(z)
    in_specs.append(pl.BlockSpec((None, None, s, LANE), lambda bi, h: (bi, h, 0, 0)))

    assert sum(dl == 1 for _, dl in patterns) == 1 and npat > 1
    park_rows = [(s // dl) * _park_pitch(dl) for _, dl in patterns if dl != 1]
    est = (2 * 3 * npat * s * LANE * 2 + 2 * s * LANE * 4 + 2 * s * LANE * 2
           + 3 * sum(park_rows) * LANE * 4 + 3 * npat * ATTN_TQ * tk * 4)
    return pl.pallas_call(
        functools.partial(_attn_kernel, n_heads=n_heads, patterns=patterns, seq=s),
        grid=(b, n_heads),
        in_specs=in_specs,
        out_specs=pl.BlockSpec((None, None, s, LANE), lambda bi, h: (bi, h, 0, 0)),
        out_shape=jax.ShapeDtypeStruct((b, n_heads, s, LANE), BF16),
        scratch_shapes=[pltpu.VMEM((rows, LANE), F32) for rows in park_rows for _ in range(3)]
        + [pltpu.VMEM((3 * npat, ATTN_TQ, tk), F32)],
        compiler_params=pltpu.CompilerParams(
            dimension_semantics=("arbitrary", "arbitrary"),
            vmem_limit_bytes=_vmem_limit(est)),
        name="dilated_attention",
    )(*operands)


def _log_sigmoid(v):
    return jnp.minimum(v, 0.0) - jnp.log1p(jnp.exp(-jnp.abs(v)))


def _ret_kernel(q_ref, k_ref, v_ref, z_ref, dl_ref, o_ref, oi_sc, uf_sc, ub_sc, st_sc, *, seq):
    c = RET_CHUNK
    n = seq // c
    dk = q_ref.shape[-1]
    dv = 2 * LANE

    lg_f = _log_sigmoid(dl_ref[0, 0:1, :])
    lg_b = _log_sigmoid(dl_ref[1, 0:1, :])
    ri = lax.broadcasted_iota(jnp.int32, (c, c), 0)
    ci = lax.broadcasted_iota(jnp.int32, (c, c), 1)
    rel = (ri - ci).astype(F32)
    dec = (jnp.where(rel >= 0, jnp.exp(jnp.maximum(rel, 0.0) * lg_f[:, :c]), 0.0)
           + jnp.where(rel <= 0, jnp.exp(jnp.maximum(-rel, 0.0) * lg_b[:, :c]), 0.0))
    idx_v = lax.broadcasted_iota(jnp.int32, (c, dv), 0).astype(F32)
    idx_k = lax.broadcasted_iota(jnp.int32, (c, dk), 0).astype(F32)
    xi_f = jnp.exp((idx_v + 1.0) * lg_f)
    xi_b = jnp.exp((c - idx_v) * lg_b)
    zeta_f = jnp.exp((c - 1.0 - idx_k) * lg_f[:, :dk])
    zeta_b = jnp.exp(idx_k * lg_b[:, :dk])
    g_f = jnp.exp(c * lg_f)
    g_b = jnp.exp(c * lg_b)

    def intra(i, carry):
        rows = pl.ds(pl.multiple_of(i * c, c), c)
        q = q_ref[rows, :]
        k = k_ref[rows, :]
        v = jnp.concatenate([v_ref[0, rows, :], v_ref[1, rows, :]], axis=-1)
        qk = lax.dot_general(q, k, (((1,), (1,)), ((), ())), preferred_element_type=F32)
        kf = k.astype(F32)
        lhs = jnp.concatenate([(qk * dec).astype(BF16),
                               (kf * zeta_f).T.astype(BF16),
                               (kf * zeta_b).T.astype(BF16)], axis=0)
        res = jnp.dot(lhs, v, preferred_element_type=F32)
        oi_sc[rows, :] = res[:c]
        uf_sc[i] = res[c:c + dk]
        ub_sc[i] = res[c + dk:]
        return carry

    lax.fori_loop(0, n, intra, 0, unroll=RET_UNROLL)

    def rec_f(i, st):
        st_sc[i, :, 0:dv] = st.astype(BF16)
        return st * g_f + uf_sc[i]

    def rec_b(t, st):
        i = n - 1 - t
        st_sc[i, :, dv:2 * dv] = st.astype(BF16)
        return st * g_b + ub_sc[i]

    lax.fori_loop(0, n, rec_f, jnp.zeros((dk, dv), F32))
    lax.fori_loop(0, n, rec_b, jnp.zeros((dk, dv), F32))

    def cross(i, carry):
        rows = pl.ds(pl.multiple_of(i * c, c), c)
        cr = jnp.dot(q_ref[rows, :], st_sc[i], preferred_element_type=F32)
        o = oi_sc[rows, :] + cr[:, :dv] * xi_f + cr[:, dv:] * xi_b
        y = o * lax.rsqrt(jnp.mean(o * o, axis=-1, keepdims=True) + NORM_EPS)
        z = jnp.concatenate([z_ref[0, rows, :], z_ref[1, rows, :]], axis=-1)
        out = (y * _silu(z)).astype(o_ref.dtype)
        o_ref[0, rows, :] = out[:, :LANE]
        o_ref[1, rows, :] = out[:, LANE:]
        return carry

    lax.fori_loop(0, n, cross, 0, unroll=RET_UNROLL)


def _retention(qkv, z, dl, z_blk2):
    b, _, s, _ = qkv.shape
    nh = N_HEADS_RET
    dv = 2 * LANE
    n = s // RET_CHUNK
    est = (2 * 2 * s * LANE * 2 + 2 * s * dv * 2 + 2 * s * dv * 4 + 2 * s * dv * 2
           + s * dv * 4 + 2 * n * LANE * dv * 4 + n * LANE * 2 * dv * 2)
    return pl.pallas_call(
        functools.partial(_ret_kernel, seq=s),
        grid=(b, nh),
        in_specs=[
            pl.BlockSpec((None, None, s, LANE), lambda bi, h: (bi, h, 0, 0)),
            pl.BlockSpec((None, None, s, LANE), lambda bi, h: (bi, nh + h, 0, 0)),
            pl.BlockSpec((None, 2, s, LANE), lambda bi, h: (bi, nh + h, 0, 0)),
            pl.BlockSpec((None, 2, s, LANE), lambda bi, h: (bi, z_blk2 + h, 0, 0)),
            pl.BlockSpec((None, 2, 8, dv), lambda bi, h: (h, 0, 0, 0)),
        ],
        out_specs=pl.BlockSpec((None, 2, s, LANE), lambda bi, h: (bi, h, 0, 0)),
        out_shape=jax.ShapeDtypeStruct((b, 2 * nh, s, LANE), BF16),
        scratch_shapes=[
            pltpu.VMEM((s, dv), F32),
            pltpu.VMEM((n, LANE, dv), F32),
            pltpu.VMEM((n, LANE, dv), F32),
            pltpu.VMEM((n, LANE, 2 * dv), BF16),
        ],
        compiler_params=pltpu.CompilerParams(
            dimension_semantics=("arbitrary", "arbitrary"),
            vmem_limit_bytes=_vmem_limit(est)),
        name="retention",
    )(qkv, qkv, qkv, z, dl)


def _outproj_kernel(ya_ref, yr_ref, w_ref, x_ref, mod_ref, gn_ref, modn_ref, *rest, last):
    if last:
        o_ref, wbf_ref = rest
    else:
        xo_ref, h_ref, wbf_ref = rest

    @pl.when((pl.program_id(0) == 0) & (pl.program_id(1) == 0))
    def _():
        _cast_weight(w_ref, wbf_ref)

    tm = x_ref.shape[0]
    for s0 in range(0, tm, OUT_SUB_ROWS):
        rows = slice(s0, s0 + OUT_SUB_ROWS)
        y = jnp.concatenate([ya_ref[j, rows, :] for j in range(ya_ref.shape[0])]
                            + [yr_ref[j, rows, :] for j in range(yr_ref.shape[0])], axis=-1)
        out = jnp.dot(y, wbf_ref[...], preferred_element_type=F32)
        xn = x_ref[rows, :] + mod_ref[2:3, :] * out
        if last:
            o_ref[rows, :] = (xn * lax.rsqrt(jnp.mean(xn * xn, axis=-1, keepdims=True) + NORM_EPS)
                              * gn_ref[...])
        else:
            xo_ref[rows, :] = xn
            h = _modulated_norm(xn, gn_ref[...], modn_ref[0:1, :], modn_ref[1:2, :])
            h_ref[rows, :] = h.astype(BF16)


def _outproj(ya, yr, w_out, x, mod, gains, layer, last):
    b, s, d = x.shape
    tm = OUT_ROW_TILE
    nba, nbr = ya.shape[1], yr.shape[1]
    kdim = (nba + nbr) * LANE
    nxt = layer if last else layer + 1
    est = (kdim * d * 4 + kdim * d * 2 + 2 * tm * kdim * 2 + 2 * tm * d * 4
           + 2 * tm * d * 4 + 2 * tm * d * 2 + 3 * tm * d * 4)
    if last:
        out_shape = jax.ShapeDtypeStruct((b, s, d), F32)
        out_specs = pl.BlockSpec((None, tm, d), lambda bi, m: (bi, m, 0))
    else:
        out_shape = (jax.ShapeDtypeStruct((b, s, d), F32), jax.ShapeDtypeStruct((b, s, d), BF16))
        out_specs = (pl.BlockSpec((None, tm, d), lambda bi, m: (bi, m, 0)),
                     pl.BlockSpec((None, tm, d), lambda bi, m: (bi, m, 0)))
    return pl.pallas_call(
        functools.partial(_outproj_kernel, last=last),
        grid=(b, s // tm),
        in_specs=[
            pl.BlockSpec((None, nba, tm, LANE), lambda bi, m: (bi, 0, m, 0)),
            pl.BlockSpec((None, nbr, tm, LANE), lambda bi, m: (bi, 0, m, 0)),
            pl.BlockSpec((None, kdim, d), lambda bi, m: (layer, 0, 0),
                         pipeline_mode=pl.Buffered(1)),
            pl.BlockSpec((None, tm, d), lambda bi, m: (bi, m, 0)),
            pl.BlockSpec((None, None, 3, d), lambda bi, m: (layer, bi, 0, 0)),
            pl.BlockSpec((None, 1, d), lambda bi, m: (layer + 1, 0, 0)),
            pl.BlockSpec((None, None, 3, d), lambda bi, m: (nxt, bi, 0, 0)),
        ],
        out_specs=out_specs,
        out_shape=out_shape,
        scratch_shapes=[pltpu.VMEM((kdim, d), BF16)],
        compiler_params=pltpu.CompilerParams(
            dimension_semantics=("arbitrary", "arbitrary"),
            vmem_limit_bytes=_vmem_limit(est)),
        name="outproj_last" if last else "outproj",
    )(ya, yr, w_out, x, mod, gains, mod)


def kernel(x, c, norm_gain, w_ada, b_ada, w_in, w_out, ret_decay_logit_f, ret_decay_logit_b,
           final_gain):
    b, s, d = x.shape
    depth = w_in.shape[0]
    attn_w = d // 2
    ret_w = d // 2
    n_heads = attn_w // HEAD_DIM_ATTN
    ret_v = ret_w // N_HEADS_RET
    ret_qk = ret_v // 2
    assert ret_qk == LANE and ret_v == 2 * LANE and HEAD_DIM_ATTN == LANE
    assert attn_w == IN_COL_TILE and N_HEADS_RET * ret_qk * 2 == IN_COL_TILE
    assert w_in.shape[2] == 7 * IN_COL_TILE and s % ROW_TILE == 0
    dilations = tuple(dl for _, dl in DILATION_PATTERNS)
    assert all(IN_SUB_ROWS % (dl * 16) == 0 for dl in dilations) and ROW_TILE % IN_SUB_ROWS == 0
    assert dilations[0] == 1 and all(b_ % a_ == 0 for a_, b_ in zip(dilations, dilations[1:]))

    colscale = jnp.ones((7, IN_COL_TILE), F32)
    colscale = colscale.at[0, :].set(HEAD_DIM_ATTN ** -0.5 * LOG2_E)
    colscale = colscale.at[4, N_HEADS_RET * ret_qk:].set(ret_qk ** -0.5)
    colscale = colscale.reshape(1, 7 * IN_COL_TILE)

    c_pad = jnp.zeros((C_PAD, d), F32).at[:b].set(c)
    mod = _adaln(c_pad, w_ada, b_ada).reshape(depth, C_PAD, 3, d)
    gains = jnp.concatenate([norm_gain, final_gain[None]], axis=0).reshape(depth + 1, 1, d)

    def lanes(v):
        return jnp.broadcast_to(v[:, :, None, None, None], (depth, N_HEADS_RET, 1, 8, 2 * LANE))
    dl = jnp.concatenate([lanes(ret_decay_logit_f), lanes(ret_decay_logit_b)], axis=2)

    h = _modulate(x, gains, mod, 0)
    out = None
    for layer in range(depth):
        qkv_a = _inproj_attn(h, w_in, colscale, layer, 3, lambda j: j, dilations)
        qkv_r = _inproj(h, w_in, colscale, layer, 2, lambda j: j + 4, BF16, "inproj_ret")
        z = _inproj(h, w_in, colscale, layer, 2, lambda j: 3 + 3 * j, F32, "inproj_gate")
        ya = _attention(qkv_a, z, n_heads)
        yr = _retention(qkv_r, z, dl[layer], n_heads // 2)
        last = layer == depth - 1
        res = _outproj(ya, yr, w_out, x, mod, gains, layer, last)
        if last:
            out = res
        else:
            x, h = res
    return out
```

```python
import functools

import jax
import jax.numpy as jnp
from jax import lax
from jax.experimental import pallas as pl
from jax.experimental.pallas import tpu as pltpu

F32 = jnp.float32
BF16 = jnp.bfloat16

HEAD_DIM_ATTN = 128
DILATION_PATTERNS = ((128, 1), (512, 4), (2048, 16))
N_HEADS_RET = 4
RET_CHUNK = 128
NORM_EPS = 1e-6
LOG2_E = 1.4426950408889634

LANE = 128
VMEM_CAP_BYTES = 60000 * 1024
TEMP_ALLOWANCE = 8 * 2**20

ROW_TILE = 1024
NORM_ROW_TILE = 512
OUT_ROW_TILE = 256
IN_COL_TILE = 1024
ADA_COL_TILE = 1024
ATTN_TQ = 128
ATTN_UNROLL = 32
RET_UNROLL = 32
IN_SUB_ROWS = 256
OUT_SUB_ROWS = 128
CAST_ROWS = 256
C_PAD = 16


def _vmem_limit(nbytes):
    return int(min(VMEM_CAP_BYTES, nbytes + TEMP_ALLOWANCE))


def _silu(v):
    return v * jax.nn.sigmoid(v)


def _cast_weight(w_ref, wbf_ref):
    n = w_ref.shape[0] // CAST_ROWS

    def body(i, carry):
        rows = pl.ds(pl.multiple_of(i * CAST_ROWS, CAST_ROWS), CAST_ROWS)
        wbf_ref[rows, :] = w_ref[rows, :].astype(BF16)
        return carry

    lax.fori_loop(0, n, body, 0)


def _ada_kernel(c_ref, w_ref, b_ref, o_ref):
    c_act = _silu(c_ref[...])
    acc = jnp.dot(c_act.astype(BF16), w_ref[...].astype(BF16), preferred_element_type=F32)
    o_ref[...] = acc + b_ref[...]


def _adaln(c_pad, w_ada, b_ada):
    depth, d, n3 = w_ada.shape
    tn = ADA_COL_TILE
    est = 2 * d * tn * 4 + d * tn * 2 + 4 * C_PAD * (d + 2 * tn) * 4
    return pl.pallas_call(
        _ada_kernel,
        grid=(depth, n3 // tn),
        in_specs=[
            pl.BlockSpec((C_PAD, d), lambda l, j: (0, 0)),
            pl.BlockSpec((None, d, tn), lambda l, j: (l, 0, j)),
            pl.BlockSpec((None, 1, tn), lambda l, j: (l, 0, j)),
        ],
        out_specs=pl.BlockSpec((None, C_PAD, tn), lambda l, j: (l, 0, j)),
        out_shape=jax.ShapeDtypeStruct((depth, C_PAD, n3), F32),
        compiler_params=pltpu.CompilerParams(
            dimension_semantics=("arbitrary", "arbitrary"),
            vmem_limit_bytes=_vmem_limit(est)),
        name="adaln_mod",
    )(c_pad, w_ada, b_ada.reshape(depth, 1, n3))


def _modulated_norm(x, g, shift, scale):
    y = x * lax.rsqrt(jnp.mean(x * x, axis=-1, keepdims=True) + NORM_EPS)
    return y * (g * (1.0 + scale)) + shift


def _modulate_kernel(x_ref, g_ref, mod_ref, h_ref):
    h = _modulated_norm(x_ref[...], g_ref[...], mod_ref[0:1, :], mod_ref[1:2, :])
    h_ref[...] = h.astype(BF16)


def _modulate(x, gains, mod, layer):
    b, s, d = x.shape
    tm = NORM_ROW_TILE
    est = 2 * tm * d * (4 + 2) + 3 * tm * d * 4
    return pl.pallas_call(
        _modulate_kernel,
        grid=(b, s // tm),
        in_specs=[
            pl.BlockSpec((None, tm, d), lambda bi, m: (bi, m, 0)),
            pl.BlockSpec((None, 1, d), lambda bi, m: (layer, 0, 0)),
            pl.BlockSpec((None, None, 3, d), lambda bi, m: (layer, bi, 0, 0)),
        ],
        out_specs=pl.BlockSpec((None, tm, d), lambda bi, m: (bi, m, 0)),
        out_shape=jax.ShapeDtypeStruct((b, s, d), BF16),
        compiler_params=pltpu.CompilerParams(
            dimension_semantics=("arbitrary", "arbitrary"),
            vmem_limit_bytes=_vmem_limit(est)),
        name="modulate0",
    )(x, gains, mod)


def _project_tile(h_ref, w_ref, cs_ref, wbf_ref):
    @pl.when((pl.program_id(1) == 0) & (pl.program_id(2) == 0))
    def _():
        _cast_weight(w_ref, wbf_ref)

    acc = jnp.dot(h_ref[...], wbf_ref[...], preferred_element_type=F32)
    return acc * cs_ref[...]


def _inproj_kernel(h_ref, w_ref, cs_ref, o_ref, wbf_ref):
    acc = _project_tile(h_ref, w_ref, cs_ref, wbf_ref)
    for j in range(o_ref.shape[0]):
        o_ref[j] = acc[:, j * LANE:(j + 1) * LANE].astype(o_ref.dtype)


def _inproj_attn_kernel(h_ref, w_ref, cs_ref, *rest, dilations):
    nd = len(dilations)
    out_refs = rest[:nd]
    wbf_ref = rest[nd]
    stage_refs = rest[nd + 1:]

    @pl.when((pl.program_id(1) == 0) & (pl.program_id(2) == 0))
    def _():
        _cast_weight(w_ref, wbf_ref)

    nb, sub = stage_refs[0].shape[0], stage_refs[0].shape[2]
    for t in range(len(stage_refs) // (nd - 1)):
        stages = stage_refs[t * (nd - 1):(t + 1) * (nd - 1)]
        s0 = t * sub
        acc = jnp.dot(h_ref[s0:s0 + sub, :], wbf_ref[...], preferred_element_type=F32)
        acc = acc * cs_ref[...]
        for j in range(nb):
            blk = acc[:, j * LANE:(j + 1) * LANE]
            stages[0][j, 0] = blk
            out_refs[0][j, 0, s0:s0 + sub, :] = blk.astype(out_refs[0].dtype)
        for k in range(1, nd):
            prev, dil = dilations[k - 1], dilations[k]
            q, n = dil // prev, sub // dil
            for j in range(nb):
                for r in range(dil):
                    ra, rb = r % prev, r // prev
                    v = stages[k - 1][j, ra, pl.ds(rb, n, stride=q), :]
                    out_refs[k][j, r, s0 // dil:s0 // dil + n, :] = v.astype(out_refs[k].dtype)
                    if k < nd - 1:
                        stages[k][j, r] = v


def _inproj_specs(h, layer, tile_of):
    b, s, d = h.shape
    tm, tn = ROW_TILE, IN_COL_TILE
    return [
        pl.BlockSpec((None, tm, d), lambda j, bi, m: (bi, m, 0)),
        pl.BlockSpec((None, d, tn), lambda j, bi, m: (layer, 0, tile_of(j))),
        pl.BlockSpec((1, tn), lambda j, bi, m: (0, tile_of(j))),
    ]


def _inproj(h, w_in, colscale, layer, n_tiles, tile_of, out_dtype, name):
    b, s, d = h.shape
    tm, tn = ROW_TILE, IN_COL_TILE
    nb = tn // LANE
    osz = jnp.dtype(out_dtype).itemsize
    est = 2 * d * tn * 4 + d * tn * 2 + 2 * tm * d * 2 + 2 * tm * tn * osz + 2 * tm * tn * 4
    return pl.pallas_call(
        _inproj_kernel,
        grid=(n_tiles, b, s // tm),
        in_specs=_inproj_specs(h, layer, tile_of),
        out_specs=pl.BlockSpec((None, nb, tm, LANE), lambda j, bi, m: (bi, j, m, 0)),
        out_shape=jax.ShapeDtypeStruct((b, n_tiles * nb, s, LANE), out_dtype),
        scratch_shapes=[pltpu.VMEM((d, tn), BF16)],
        compiler_params=pltpu.CompilerParams(
            dimension_semantics=("arbitrary", "arbitrary", "arbitrary"),
            vmem_limit_bytes=_vmem_limit(est)),
        name=name,
    )(h, w_in, colscale)


def _inproj_attn(h, w_in, colscale, layer, n_tiles, tile_of, dilations):
    b, s, d = h.shape
    tm, tn = ROW_TILE, IN_COL_TILE
    nb = tn // LANE
    nd = len(dilations)
    est = (2 * d * tn * 4 + d * tn * 2 + 2 * tm * d * 2 + nd * 2 * tm * tn * 2
           + 3 * tm * tn * 4)
    return pl.pallas_call(
        functools.partial(_inproj_attn_kernel, dilations=dilations),
        grid=(n_tiles, b, s // tm),
        in_specs=_inproj_specs(h, layer, tile_of),
        out_specs=[pl.BlockSpec((None, nb, dl, tm // dl, LANE),
                                lambda j, bi, m: (bi, j, 0, m, 0)) for dl in dilations],
        out_shape=[jax.ShapeDtypeStruct((b, n_tiles * nb, dl, s // dl, LANE), BF16)
                   for dl in dilations],
        scratch_shapes=[pltpu.VMEM((d, tn), BF16)]
        + [pltpu.VMEM((nb, dl, IN_SUB_ROWS // dl, LANE), F32)
           for dl in dilations[:-1]] * (tm // IN_SUB_ROWS),
        compiler_params=pltpu.CompilerParams(
            dimension_semantics=("arbitrary", "arbitrary", "arbitrary"),
            vmem_limit_bytes=_vmem_limit(est)),
        name="inproj_attn",
    )(h, w_in, colscale)


def _park_pitch(dil):
    return dil + 4 if dil % 16 == 0 else dil


def _attn_kernel(*refs, n_heads, patterns, seq):
    npat = len(patterns)
    qkv_refs = refs[:3 * npat]
    z_ref, o_ref = refs[3 * npat:3 * npat + 2]
    bias_sc = refs[-1]
    park = refs[3 * npat + 2:-1]
    tq = ATTN_TQ
    radius = patterns[0][0] // (2 * patterns[0][1])
    tk = tq + 2 * radius

    head = pl.program_id(1)
    expo = (head + 1).astype(F32) * (-8.0 / n_heads)
    slope = jnp.exp2(jnp.full((tq, tk), 1.0, F32) * expo)
    row = lax.broadcasted_iota(jnp.int32, (tq, tk), 0)
    col = lax.broadcasted_iota(jnp.int32, (tq, tk), 1)
    for p, (_, dil) in enumerate(patterns):
        for case in range(3):
            rel = jnp.abs(col - row - case * radius)
            bias = -slope * (rel * dil).astype(F32) * LOG2_E
            bias_sc[p * 3 + case] = jnp.where(rel <= radius, bias, -jnp.inf)

    nblocks = seq // tq
    strided = [p for p in range(npat) if patterns[p][1] != 1]
    slot_of = {p: i for i, p in enumerate(strided)}
    for p in strided + [p for p in range(npat) if patterns[p][1] == 1]:
        dil = patterns[p][1]
        sub_len = seq // dil
        nblk = sub_len // tq
        shift = nblk.bit_length() - 1
        q_ref, k_ref, v_ref = qkv_refs[3 * p:3 * p + 3]

        def block(g, carry, p=p, dil=dil, sub_len=sub_len, nblk=nblk, shift=shift,
                  q_ref=q_ref, k_ref=k_ref, v_ref=v_ref):
            r = lax.shift_right_logical(g, shift)
            i = lax.bitwise_and(g, nblk - 1)
            l0 = pl.multiple_of(i * tq, tq)
            ks = pl.multiple_of(jnp.clip(l0 - radius, 0, sub_len - tk), radius)
            case = (l0 - ks) // radius
            q = q_ref[r, pl.ds(l0, tq), :]
            k = k_ref[r, pl.ds(ks, tk), :]
            v = v_ref[r, pl.ds(ks, tk), :]
            s = lax.dot_general(q, k, (((1,), (1,)), ((), ())), preferred_element_type=F32)
            s = s + bias_sc[p * 3 + case]
            m = jnp.max(s, axis=-1, keepdims=True)
            pe = jnp.exp2(s - m)
            den = jnp.sum(pe, axis=-1, keepdims=True)
            acc = jnp.dot(pe.astype(BF16), v, preferred_element_type=F32)
            if dil != 1:
                pitch = _park_pitch(dil)
                rows = pl.ds(l0 * pitch + r, tq, stride=pitch)
                acc_ref, m_ref, den_ref = park[3 * slot_of[p]:3 * slot_of[p] + 3]
                acc_ref[rows, :] = acc
                m_ref[rows, :] = jnp.broadcast_to(m, (tq, LANE))
                den_ref[rows, :] = jnp.broadcast_to(den, (tq, LANE))
                return carry

            def parked(ref, dl):
                pitch = _park_pitch(dl)
                if pitch == dl:
                    return ref[pl.ds(l0, tq), :]
                first = l0 // dl
                return jnp.concatenate(
                    [ref[pl.ds(pl.multiple_of((first + g) * pitch, 4), dl), :]
                     for g in range(tq // dl)], axis=0)

            dils = [patterns[i][1] for i in strided]
            others = [parked(park[3 * i + 1], dl) for i, dl in enumerate(dils)]
            mx = jnp.maximum(functools.reduce(jnp.maximum, others), m)
            e = jnp.exp2(m - mx)
            num = e * acc
            wsum = e * den
            for i, (ms, dl) in enumerate(zip(others, dils)):
                e = jnp.exp2(ms - mx)
                num = num + e * parked(park[3 * i], dl)
                wsum = wsum + e * parked(park[3 * i + 2], dl)
            rows = pl.ds(l0, tq)
            o_ref[rows, :] = ((num / wsum) * _silu(z_ref[rows, :])).astype(o_ref.dtype)
            return carry

        lax.fori_loop(0, nblocks, block, 0, unroll=ATTN_UNROLL)


def _attention(qkv_views, z, n_heads):
    patterns = DILATION_PATTERNS
    b, _, _, s, _ = qkv_views[0].shape
    radius = patterns[0][0] // (2 * patterns[0][1])
    assert all(w // (2 * dl) == radius for w, dl in patterns)
    tk = ATTN_TQ + 2 * radius
    for _, dl in patterns:
        nblk = (s // dl) // ATTN_TQ
        assert (s // dl) % ATTN_TQ == 0 and s // dl >= tk and nblk & (nblk - 1) == 0
    assert (s // ATTN_TQ) % ATTN_UNROLL == 0
    npat = len(patterns)

    operands, in_specs = [], []
    for (_, dl), view in zip(patterns, qkv_views):
        for t in range(3):
            operands.append(view)
            in_specs.append(pl.BlockSpec(
                (None, None, dl, s // dl, LANE),
                lambda bi, h, t=t: (bi, t * n_heads + h, 0, 0, 0)))
    operands.append(z)
    in_specs.append(pl.BlockSpec((None, None, s, LANE), lambda bi, h: (bi, h, 0, 0)))

    assert sum(dl == 1 for _, dl in patterns) == 1 and npat > 1
    park_rows = [(s // dl) * _park_pitch(dl) for _, dl in patterns if dl != 1]
    est = (2 * 3 * npat * s * LANE * 2 + 2 * s * LANE * 4 + 2 * s * LANE * 2
           + 3 * sum(park_rows) * LANE * 4 + 3 * npat * ATTN_TQ * tk * 4)
    return pl.pallas_call(
        functools.partial(_attn_kernel, n_heads=n_heads, patterns=patterns, seq=s),
        grid=(b, n_heads),
        in_specs=in_specs,
        out_specs=pl.BlockSpec((None, None, s, LANE), lambda bi, h: (bi, h, 0, 0)),
        out_shape=jax.ShapeDtypeStruct((b, n_heads, s, LANE), BF16),
        scratch_shapes=[pltpu.VMEM((rows, LANE), F32) for rows in park_rows for _ in range(3)]
        + [pltpu.VMEM((3 * npat, ATTN_TQ, tk), F32)],
        compiler_params=pltpu.CompilerParams(
            dimension_semantics=("arbitrary", "arbitrary"),
            vmem_limit_bytes=_vmem_limit(est)),
        name="dilated_attention",
    )(*operands)


def _log_sigmoid(v):
    return jnp.minimum(v, 0.0) - jnp.log1p(jnp.exp(-jnp.abs(v)))


def _ret_kernel(q_ref, k_ref, v_ref, z_ref, dl_ref, o_ref, oi_sc, uf_sc, ub_sc, st_sc, *, seq):
    c = RET_CHUNK
    n = seq // c
    dk = q_ref.shape[-1]
    dv = 2 * LANE

    lg_f = _log_sigmoid(dl_ref[0, 0:1, :])
    lg_b = _log_sigmoid(dl_ref[1, 0:1, :])
    ri = lax.broadcasted_iota(jnp.int32, (c, c), 0)
    ci = lax.broadcasted_iota(jnp.int32, (c, c), 1)
    rel = (ri - ci).astype(F32)
    dec = (jnp.where(rel >= 0, jnp.exp(jnp.maximum(rel, 0.0) * lg_f[:, :c]), 0.0)
           + jnp.where(rel <= 0, jnp.exp(jnp.maximum(-rel, 0.0) * lg_b[:, :c]), 0.0))
    idx_v = lax.broadcasted_iota(jnp.int32, (c, dv), 0).astype(F32)
    idx_k = lax.broadcasted_iota(jnp.int32, (c, dk), 0).astype(F32)
    xi_f = jnp.exp((idx_v + 1.0) * lg_f)
    xi_b = jnp.exp((c - idx_v) * lg_b)
    zeta_f = jnp.exp((c - 1.0 - idx_k) * lg_f[:, :dk])
    zeta_b = jnp.exp(idx_k * lg_b[:, :dk])
    g_f = jnp.exp(c * lg_f)
    g_b = jnp.exp(c * lg_b)

    def intra(i, carry):
        rows = pl.ds(pl.multiple_of(i * c, c), c)
        q = q_ref[rows, :]
        k = k_ref[rows, :]
        v = jnp.concatenate([v_ref[0, rows, :], v_ref[1, rows, :]], axis=-1)
        qk = lax.dot_general(q, k, (((1,), (1,)), ((), ())), preferred_element_type=F32)
        kf = k.astype(F32)
        lhs = jnp.concatenate([(qk * dec).astype(BF16),
                               (kf * zeta_f).T.astype(BF16),
                               (kf * zeta_b).T.astype(BF16)], axis=0)
        res = jnp.dot(lhs, v, preferred_element_type=F32)
        oi_sc[rows, :] = res[:c]
        uf_sc[i] = res[c:c + dk]
        ub_sc[i] = res[c + dk:]
        return carry

    lax.fori_loop(0, n, intra, 0, unroll=RET_UNROLL)

    def rec_f(i, st):
        st_sc[i, :, 0:dv] = st.astype(BF16)
        return st * g_f + uf_sc[i]

    def rec_b(t, st):
        i = n - 1 - t
        st_sc[i, :, dv:2 * dv] = st.astype(BF16)
        return st * g_b + ub_sc[i]

    lax.fori_loop(0, n, rec_f, jnp.zeros((dk, dv), F32))
    lax.fori_loop(0, n, rec_b, jnp.zeros((dk, dv), F32))

    def cross(i, carry):
        rows = pl.ds(pl.multiple_of(i * c, c), c)
        cr = jnp.dot(q_ref[rows, :], st_sc[i], preferred_element_type=F32)
        o = oi_sc[rows, :] + cr[:, :dv] * xi_f + cr[:, dv:] * xi_b
        y = o * lax.rsqrt(jnp.mean(o * o, axis=-1, keepdims=True) + NORM_EPS)
        z = jnp.concatenate([z_ref[0, rows, :], z_ref[1, rows, :]], axis=-1)
        out = (y * _silu(z)).astype(o_ref.dtype)
        o_ref[0, rows, :] = out[:, :LANE]
        o_ref[1, rows, :] = out[:, LANE:]
        return carry

    lax.fori_loop(0, n, cross, 0, unroll=RET_UNROLL)


def _retention(qkv, z, dl, z_blk2):
    b, _, s, _ = qkv.shape
    nh = N_HEADS_RET
    dv = 2 * LANE
    n = s // RET_CHUNK
    est = (2 * 2 * s * LANE * 2 + 2 * s * dv * 2 + 2 * s * dv * 4 + 2 * s * dv * 2
           + s * dv * 4 + 2 * n * LANE * dv * 4 + n * LANE * 2 * dv * 2)
    return pl.pallas_call(
        functools.partial(_ret_kernel, seq=s),
        grid=(b, nh),
        in_specs=[
            pl.BlockSpec((None, None, s, LANE), lambda bi, h: (bi, h, 0, 0)),
            pl.BlockSpec((None, None, s, LANE), lambda bi, h: (bi, nh + h, 0, 0)),
            pl.BlockSpec((None, 2, s, LANE), lambda bi, h: (bi, nh + h, 0, 0)),
            pl.BlockSpec((None, 2, s, LANE), lambda bi, h: (bi, z_blk2 + h, 0, 0)),
            pl.BlockSpec((None, 2, 8, dv), lambda bi, h: (h, 0, 0, 0)),
        ],
        out_specs=pl.BlockSpec((None, 2, s, LANE), lambda bi, h: (bi, h, 0, 0)),
        out_shape=jax.ShapeDtypeStruct((b, 2 * nh, s, LANE), BF16),
        scratch_shapes=[
            pltpu.VMEM((s, dv), F32),
            pltpu.VMEM((n, LANE, dv), F32),
            pltpu.VMEM((n, LANE, dv), F32),
            pltpu.VMEM((n, LANE, 2 * dv), BF16),
        ],
        compiler_params=pltpu.CompilerParams(
            dimension_semantics=("arbitrary", "arbitrary"),
            vmem_limit_bytes=_vmem_limit(est)),
        name="retention",
    )(qkv, qkv, qkv, z, dl)


def _outproj_kernel(ya_ref, yr_ref, w_ref, x_ref, mod_ref, gn_ref, modn_ref, *rest, last):
    if last:
        o_ref, wbf_ref = rest
    else:
        xo_ref, h_ref, wbf_ref = rest

    @pl.when((pl.program_id(0) == 0) & (pl.program_id(1) == 0))
    def _():
        _cast_weight(w_ref, wbf_ref)

    tm = x_ref.shape[0]
    for s0 in range(0, tm, OUT_SUB_ROWS):
        rows = slice(s0, s0 + OUT_SUB_ROWS)
        y = jnp.concatenate([ya_ref[j, rows, :] for j in range(ya_ref.shape[0])]
                            + [yr_ref[j, rows, :] for j in range(yr_ref.shape[0])], axis=-1)
        out = jnp.dot(y, wbf_ref[...], preferred_element_type=F32)
        xn = x_ref[rows, :] + mod_ref[2:3, :] * out
        if last:
            o_ref[rows, :] = (xn * lax.rsqrt(jnp.mean(xn * xn, axis=-1, keepdims=True) + NORM_EPS)
                              * gn_ref[...])
        else:
            xo_ref[rows, :] = xn
            h = _modulated_norm(xn, gn_ref[...], modn_ref[0:1, :], modn_ref[1:2, :])
            h_ref[rows, :] = h.astype(BF16)


def _outproj(ya, yr, w_out, x, mod, gains, layer, last):
    b, s, d = x.shape
    tm = OUT_ROW_TILE
    nba, nbr = ya.shape[1], yr.shape[1]
    kdim = (nba + nbr) * LANE
    nxt = layer if last else layer + 1
    est = (kdim * d * 4 + kdim * d * 2 + 2 * tm * kdim * 2 + 2 * tm * d * 4
           + 2 * tm * d * 4 + 2 * tm * d * 2 + 3 * tm * d * 4)
    if last:
        out_shape = jax.ShapeDtypeStruct((b, s, d), F32)
        out_specs = pl.BlockSpec((None, tm, d), lambda bi, m: (bi, m, 0))
    else:
        out_shape = (jax.ShapeDtypeStruct((b, s, d), F32), jax.ShapeDtypeStruct((b, s, d), BF16))
        out_specs = (pl.BlockSpec((None, tm, d), lambda bi, m: (bi, m, 0)),
                     pl.BlockSpec((None, tm, d), lambda bi, m: (bi, m, 0)))
    return pl.pallas_call(
        functools.partial(_outproj_kernel, last=last),
        grid=(b, s // tm),
        in_specs=[
            pl.BlockSpec((None, nba, tm, LANE), lambda bi, m: (bi, 0, m, 0)),
            pl.BlockSpec((None, nbr, tm, LANE), lambda bi, m: (bi, 0, m, 0)),
            pl.BlockSpec((None, kdim, d), lambda bi, m: (layer, 0, 0),
                         pipeline_mode=pl.Buffered(1)),
            pl.BlockSpec((None, tm, d), lambda bi, m: (bi, m, 0)),
            pl.BlockSpec((None, None, 3, d), lambda bi, m: (layer, bi, 0, 0)),
            pl.BlockSpec((None, 1, d), lambda bi, m: (layer + 1, 0, 0)),
            pl.BlockSpec((None, None, 3, d), lambda bi, m: (nxt, bi, 0, 0)),
        ],
        out_specs=out_specs,
        out_shape=out_shape,
        scratch_shapes=[pltpu.VMEM((kdim, d), BF16)],
        compiler_params=pltpu.CompilerParams(
            dimension_semantics=("arbitrary", "arbitrary"),
            vmem_limit_bytes=_vmem_limit(est)),
        name="outproj_last" if last else "outproj",
    )(ya, yr, w_out, x, mod, gains, mod)


def kernel(x, c, norm_gain, w_ada, b_ada, w_in, w_out, ret_decay_logit_f, ret_decay_logit_b,
           final_gain):
    b, s, d = x.shape
    depth = w_in.shape[0]
    attn_w = d // 2
    ret_w = d // 2
    n_heads = attn_w // HEAD_DIM_ATTN
    ret_v = ret_w // N_HEADS_RET
    ret_qk = ret_v // 2
    assert ret_qk == LANE and ret_v == 2 * LANE and HEAD_DIM_ATTN == LANE
    assert attn_w == IN_COL_TILE and N_HEADS_RET * ret_qk * 2 == IN_COL_TILE
    assert w_in.shape[2] == 7 * IN_COL_TILE and s % ROW_TILE == 0
    dilations = tuple(dl for _, dl in DILATION_PATTERNS)
    assert all(IN_SUB_ROWS % (dl * 16) == 0 for dl in dilations) and ROW_TILE % IN_SUB_ROWS == 0
    assert dilations[0] == 1 and all(b_ % a_ == 0 for a_, b_ in zip(dilations, dilations[1:]))

    colscale = jnp.ones((7, IN_COL_TILE), F32)
    colscale = colscale.at[0, :].set(HEAD_DIM_ATTN ** -0.5 * LOG2_E)
    colscale = colscale.at[4, N_HEADS_RET * ret_qk:].set(ret_qk ** -0.5)
    colscale = colscale.reshape(1, 7 * IN_COL_TILE)

    c_pad = jnp.zeros((C_PAD, d), F32).at[:b].set(c)
    mod = _adaln(c_pad, w_ada, b_ada).reshape(depth, C_PAD, 3, d)
    gains = jnp.concatenate([norm_gain, final_gain[None]], axis=0).reshape(depth + 1, 1, d)

    def lanes(v):
        return jnp.broadcast_to(v[:, :, None, None, None], (depth, N_HEADS_RET, 1, 8, 2 * LANE))
    dl = jnp.concatenate([lanes(ret_decay_logit_f), lanes(ret_decay_logit_b)], axis=2)

    h = _modulate(x, gains, mod, 0)
    out = None
    for layer in range(depth):
        qkv_a = _inproj_attn(h, w_in, colscale, layer, 3, lambda j: j, dilations)
        qkv_r = _inproj(h, w_in, colscale, layer, 2, lambda j: j + 4, BF16, "inproj_ret")
        z = _inproj(h, w_in, colscale, layer, 2, lambda j: 3 + 3 * j, F32, "inproj_gate")
        ya = _attention(qkv_a, z, n_heads)
        yr = _retention(qkv_r, z, dl[layer], n_heads // 2)
        last = layer == depth - 1
        res = _outproj(ya, yr, w_out, x, mod, gains, layer, last)
        if last:
            out = res
        else:
            x, h = res
    return out
```

```python
import functools

import jax
import jax.numpy as jnp
from jax import lax
from jax.experimental import pallas as pl
from jax.experimental.pallas import tpu as pltpu

F32 = jnp.float32
BF16 = jnp.bfloat16

HEAD_DIM_ATTN = 128
DILATION_PATTERNS = ((128, 1), (512, 4), (2048, 16))
N_HEADS_RET = 4
RET_CHUNK = 128
NORM_EPS = 1e-6
LOG2_E = 1.4426950408889634

LANE = 128
VMEM_CAP_BYTES = 60000 * 1024
TEMP_ALLOWANCE = 8 * 2**20

ROW_TILE = 1024
NORM_ROW_TILE = 512
OUT_ROW_TILE = 512
IN_COL_TILE = 1024
ADA_COL_TILE = 1024
ATTN_TQ = 128
ATTN_UNROLL = 32
RET_UNROLL = 32
IN_SUB_ROWS = 256
OUT_SUB_ROWS = 128
CAST_ROWS = 256
C_PAD = 16


def _vmem_limit(nbytes):
    return int(min(VMEM_CAP_BYTES, nbytes + TEMP_ALLOWANCE))


def _silu(v):
    return v * jax.nn.sigmoid(v)


def _cast_weight(w_ref, wbf_ref):
    n = w_ref.shape[0] // CAST_ROWS

    def body(i, carry):
        rows = pl.ds(pl.multiple_of(i * CAST_ROWS, CAST_ROWS), CAST_ROWS)
        wbf_ref[rows, :] = w_ref[rows, :].astype(BF16)
        return carry

    lax.fori_loop(0, n, body, 0)


def _ada_kernel(c_ref, w_ref, b_ref, o_ref):
    c_act = _silu(c_ref[...])
    acc = jnp.dot(c_act.astype(BF16), w_ref[...].astype(BF16), preferred_element_type=F32)
    o_ref[...] = acc + b_ref[...]


def _adaln(c_pad, w_ada, b_ada):
    depth, d, n3 = w_ada.shape
    tn = ADA_COL_TILE
    est = 2 * d * tn * 4 + d * tn * 2 + 4 * C_PAD * (d + 2 * tn) * 4
    return pl.pallas_call(
        _ada_kernel,
        grid=(depth, n3 // tn),
        in_specs=[
            pl.BlockSpec((C_PAD, d), lambda l, j: (0, 0)),
            pl.BlockSpec((None, d, tn), lambda l, j: (l, 0, j)),
            pl.BlockSpec((None, 1, tn), lambda l, j: (l, 0, j)),
        ],
        out_specs=pl.BlockSpec((None, C_PAD, tn), lambda l, j: (l, 0, j)),
        out_shape=jax.ShapeDtypeStruct((depth, C_PAD, n3), F32),
        compiler_params=pltpu.CompilerParams(
            dimension_semantics=("arbitrary", "arbitrary"),
            vmem_limit_bytes=_vmem_limit(est)),
        name="adaln_mod",
    )(c_pad, w_ada, b_ada.reshape(depth, 1, n3))


def _modulated_norm(x, g, shift, scale):
    y = x * lax.rsqrt(jnp.mean(x * x, axis=-1, keepdims=True) + NORM_EPS)
    return y * (g * (1.0 + scale)) + shift


def _modulate_kernel(x_ref, g_ref, mod_ref, h_ref):
    h = _modulated_norm(x_ref[...], g_ref[...], mod_ref[0:1, :], mod_ref[1:2, :])
    h_ref[...] = h.astype(BF16)


def _modulate(x, gains, mod, layer):
    b, s, d = x.shape
    tm = NORM_ROW_TILE
    est = 2 * tm * d * (4 + 2) + 3 * tm * d * 4
    return pl.pallas_call(
        _modulate_kernel,
        grid=(b, s // tm),
        in_specs=[
            pl.BlockSpec((None, tm, d), lambda bi, m: (bi, m, 0)),
            pl.BlockSpec((None, 1, d), lambda bi, m: (layer, 0, 0)),
            pl.BlockSpec((None, None, 3, d), lambda bi, m: (layer, bi, 0, 0)),
        ],
        out_specs=pl.BlockSpec((None, tm, d), lambda bi, m: (bi, m, 0)),
        out_shape=jax.ShapeDtypeStruct((b, s, d), BF16),
        compiler_params=pltpu.CompilerParams(
            dimension_semantics=("arbitrary", "arbitrary"),
            vmem_limit_bytes=_vmem_limit(est)),
        name="modulate0",
    )(x, gains, mod)


def _project_tile(h_ref, w_ref, cs_ref, wbf_ref):
    @pl.when((pl.program_id(1) == 0) & (pl.program_id(2) == 0))
    def _():
        _cast_weight(w_ref, wbf_ref)

    acc = jnp.dot(h_ref[...], wbf_ref[...], preferred_element_type=F32)
    return acc * cs_ref[...]


def _inproj_kernel(h_ref, w_ref, cs_ref, o_ref, wbf_ref):
    acc = _project_tile(h_ref, w_ref, cs_ref, wbf_ref)
    for j in range(o_ref.shape[0]):
        o_ref[j] = acc[:, j * LANE:(j + 1) * LANE].astype(o_ref.dtype)


def _inproj_attn_kernel(h_ref, w_ref, cs_ref, *rest, dilations):
    nd = len(dilations)
    out_refs = rest[:nd]
    wbf_ref = rest[nd]
    stage_refs = rest[nd + 1:]

    @pl.when((pl.program_id(1) == 0) & (pl.program_id(2) == 0))
    def _():
        _cast_weight(w_ref, wbf_ref)

    nb, sub = stage_refs[0].shape[0], stage_refs[0].shape[2]
    for t in range(len(stage_refs) // (nd - 1)):
        stages = stage_refs[t * (nd - 1):(t + 1) * (nd - 1)]
        s0 = t * sub
        acc = jnp.dot(h_ref[s0:s0 + sub, :], wbf_ref[...], preferred_element_type=F32)
        acc = acc * cs_ref[...]
        for j in range(nb):
            blk = acc[:, j * LANE:(j + 1) * LANE]
            stages[0][j, 0] = blk
            out_refs[0][j, 0, s0:s0 + sub, :] = blk.astype(out_refs[0].dtype)
        for k in range(1, nd):
            prev, dil = dilations[k - 1], dilations[k]
            q, n = dil // prev, sub // dil
            for j in range(nb):
                for r in range(dil):
                    ra, rb = r % prev, r // prev
                    v = stages[k - 1][j, ra, pl.ds(rb, n, stride=q), :]
                    out_refs[k][j, r, s0 // dil:s0 // dil + n, :] = v.astype(out_refs[k].dtype)
                    if k < nd - 1:
                        stages[k][j, r] = v


def _inproj_specs(h, layer, tile_of):
    b, s, d = h.shape
    tm, tn = ROW_TILE, IN_COL_TILE
    return [
        pl.BlockSpec((None, tm, d), lambda j, bi, m: (bi, m, 0)),
        pl.BlockSpec((None, d, tn), lambda j, bi, m: (layer, 0, tile_of(j))),
        pl.BlockSpec((1, tn), lambda j, bi, m: (0, tile_of(j))),
    ]


def _inproj(h, w_in, colscale, layer, n_tiles, tile_of, out_dtype, name):
    b, s, d = h.shape
    tm, tn = ROW_TILE, IN_COL_TILE
    nb = tn // LANE
    osz = jnp.dtype(out_dtype).itemsize
    est = 2 * d * tn * 4 + d * tn * 2 + 2 * tm * d * 2 + 2 * tm * tn * osz + 2 * tm * tn * 4
    return pl.pallas_call(
        _inproj_kernel,
        grid=(n_tiles, b, s // tm),
        in_specs=_inproj_specs(h, layer, tile_of),
        out_specs=pl.BlockSpec((None, nb, tm, LANE), lambda j, bi, m: (bi, j, m, 0)),
        out_shape=jax.ShapeDtypeStruct((b, n_tiles * nb, s, LANE), out_dtype),
        scratch_shapes=[pltpu.VMEM((d, tn), BF16)],
        compiler_params=pltpu.CompilerParams(
            dimension_semantics=("arbitrary", "arbitrary", "arbitrary"),
            vmem_limit_bytes=_vmem_limit(est)),
        name=name,
    )(h, w_in, colscale)


def _inproj_attn(h, w_in, colscale, layer, n_tiles, tile_of, dilations):
    b, s, d = h.shape
    tm, tn = ROW_TILE, IN_COL_TILE
    nb = tn // LANE
    nd = len(dilations)
    est = (2 * d * tn * 4 + d * tn * 2 + 2 * tm * d * 2 + nd * 2 * tm * tn * 2
           + 3 * tm * tn * 4)
    return pl.pallas_call(
        functools.partial(_inproj_attn_kernel, dilations=dilations),
        grid=(n_tiles, b, s // tm),
        in_specs=_inproj_specs(h, layer, tile_of),
        out_specs=[pl.BlockSpec((None, nb, dl, tm // dl, LANE),
                                lambda j, bi, m: (bi, j, 0, m, 0)) for dl in dilations],
        out_shape=[jax.ShapeDtypeStruct((b, n_tiles * nb, dl, s // dl, LANE), BF16)
                   for dl in dilations],
        scratch_shapes=[pltpu.VMEM((d, tn), BF16)]
        + [pltpu.VMEM((nb, dl, IN_SUB_ROWS // dl, LANE), F32)
           for dl in dilations[:-1]] * (tm // IN_SUB_ROWS),
        compiler_params=pltpu.CompilerParams(
            dimension_semantics=("arbitrary", "arbitrary", "arbitrary"),
            vmem_limit_bytes=_vmem_limit(est)),
        name="inproj_attn",
    )(h, w_in, colscale)


def _park_pitch(dil):
    return dil + 4 if dil % 16 == 0 else dil


def _attn_kernel(*refs, n_heads, patterns, seq):
    npat = len(patterns)
    qkv_refs = refs[:3 * npat]
    z_ref, o_ref = refs[3 * npat:3 * npat + 2]
    bias_sc = refs[-1]
    park = refs[3 * npat + 2:-1]
    tq = ATTN_TQ
    radius = patterns[0][0] // (2 * patterns[0][1])
    tk = tq + 2 * radius

    head = pl.program_id(1)
    expo = (head + 1).astype(F32) * (-8.0 / n_heads)
    slope = jnp.exp2(jnp.full((tq, tk), 1.0, F32) * expo)
    row = lax.broadcasted_iota(jnp.int32, (tq, tk), 0)
    col = lax.broadcasted_iota(jnp.int32, (tq, tk), 1)
    for p, (_, dil) in enumerate(patterns):
        for case in range(3):
            rel = jnp.abs(col - row - case * radius)
            bias = -slope * (rel * dil).astype(F32) * LOG2_E
            bias_sc[p * 3 + case] = jnp.where(rel <= radius, bias, -jnp.inf)

    nblocks = seq // tq
    strided = [p for p in range(npat) if patterns[p][1] != 1]
    slot_of = {p: i for i, p in enumerate(strided)}
    for p in strided + [p for p in range(npat) if patterns[p][1] == 1]:
        dil = patterns[p][1]
        sub_len = seq // dil
        nblk = sub_len // tq
        shift = nblk.bit_length() - 1
        q_ref, k_ref, v_ref = qkv_refs[3 * p:3 * p + 3]

        def block(g, carry, p=p, dil=dil, sub_len=sub_len, nblk=nblk, shift=shift,
                  q_ref=q_ref, k_ref=k_ref, v_ref=v_ref):
            r = lax.shift_right_logical(g, shift)
            i = lax.bitwise_and(g, nblk - 1)
            l0 = pl.multiple_of(i * tq, tq)
            ks = pl.multiple_of(jnp.clip(l0 - radius, 0, sub_len - tk), radius)
            case = (l0 - ks) // radius
            q = q_ref[r, pl.ds(l0, tq), :]
            k = k_ref[r, pl.ds(ks, tk), :]
            v = v_ref[r, pl.ds(ks, tk), :]
            s = lax.dot_general(q, k, (((1,), (1,)), ((), ())), preferred_element_type=F32)
            s = s + bias_sc[p * 3 + case]
            m = jnp.max(s, axis=-1, keepdims=True)
            pe = jnp.exp2(s - m)
            den = jnp.sum(pe, axis=-1, keepdims=True)
            acc = jnp.dot(pe.astype(BF16), v, preferred_element_type=F32)
            if dil != 1:
                pitch = _park_pitch(dil)
                rows = pl.ds(l0 * pitch + r, tq, stride=pitch)
                acc_ref, m_ref, den_ref = park[3 * slot_of[p]:3 * slot_of[p] + 3]
                acc_ref[rows, :] = acc
                m_ref[rows, :] = jnp.broadcast_to(m, (tq, LANE))
                den_ref[rows, :] = jnp.broadcast_to(den, (tq, LANE))
                return carry

            def parked(ref, dl):
                pitch = _park_pitch(dl)
                if pitch == dl:
                    return ref[pl.ds(l0, tq), :]
                first = l0 // dl
                return jnp.concatenate(
                    [ref[pl.ds(pl.multiple_of((first + g) * pitch, 4), dl), :]
                     for g in range(tq // dl)], axis=0)

            dils = [patterns[i][1] for i in strided]
            others = [parked(park[3 * i + 1], dl) for i, dl in enumerate(dils)]
            mx = jnp.maximum(functools.reduce(jnp.maximum, others), m)
            e = jnp.exp2(m - mx)
            num = e * acc
            wsum = e * den
            for i, (ms, dl) in enumerate(zip(others, dils)):
                e = jnp.exp2(ms - mx)
                num = num + e * parked(park[3 * i], dl)
                wsum = wsum + e * parked(park[3 * i + 2], dl)
            rows = pl.ds(l0, tq)
            o_ref[rows, :] = ((num / wsum) * _silu(z_ref[rows, :])).astype(o_ref.dtype)
            return carry

        lax.fori_loop(0, nblocks, block, 0, unroll=ATTN_UNROLL)


def _attention(qkv_views, z, n_heads):
    patterns = DILATION_PATTERNS
    b, _, _, s, _ = qkv_views[0].shape
    radius = patterns[0][0] // (2 * patterns[0][1])
    assert all(w // (2 * dl) == radius for w, dl in patterns)
    tk = ATTN_TQ + 2 * radius
    for _, dl in patterns:
        nblk = (s // dl) // ATTN_TQ
        assert (s // dl) % ATTN_TQ == 0 and s // dl >= tk and nblk & (nblk - 1) == 0
    assert (s // ATTN_TQ) % ATTN_UNROLL == 0
    npat = len(patterns)

    operands, in_specs = [], []
    for (_, dl), view in zip(patterns, qkv_views):
        for t in range(3):
            operands.append(view)
            in_specs.append(pl.BlockSpec(
                (None, None, dl, s // dl, LANE),
                lambda bi, h, t=t: (bi, t * n_heads + h, 0, 0, 0)))
    operands.append(z)
    in_specs.append(pl.BlockSpec((None, None, s, LANE), lambda bi, h: (bi, h, 0, 0)))

    assert sum(dl == 1 for _, dl in patterns) == 1 and npat > 1
    park_rows = [(s // dl) * _park_pitch(dl) for _, dl in patterns if dl != 1]
    est = (2 * 3 * npat * s * LANE * 2 + 2 * s * LANE * 4 + 2 * s * LANE * 2
           + 3 * sum(park_rows) * LANE * 4 + 3 * npat * ATTN_TQ * tk * 4)
    return pl.pallas_call(
        functools.partial(_attn_kernel, n_heads=n_heads, patterns=patterns, seq=s),
        grid=(b, n_heads),
        in_specs=in_specs,
        out_specs=pl.BlockSpec((None, None, s, LANE), lambda bi, h: (bi, h, 0, 0)),
        out_shape=jax.ShapeDtypeStruct((b, n_heads, s, LANE), BF16),
        scratch_shapes=[pltpu.VMEM((rows, LANE), F32) for rows in park_rows for _ in range(3)]
        + [pltpu.VMEM((3 * npat, ATTN_TQ, tk), F32)],
        compiler_params=pltpu.CompilerParams(
            dimension_semantics=("arbitrary", "arbitrary"),
            vmem_limit_bytes=_vmem_limit(est)),
        name="dilated_attention",
    )(*operands)


def _log_sigmoid(v):
    return jnp.minimum(v, 0.0) - jnp.log1p(jnp.exp(-jnp.abs(v)))


def _ret_kernel(q_ref, k_ref, v_ref, z_ref, dl_ref, o_ref, oi_sc, uf_sc, ub_sc, st_sc, *, seq):
    c = RET_CHUNK
    n = seq // c
    dk = q_ref.shape[-1]
    dv = 2 * LANE

    lg_f = _log_sigmoid(dl_ref[0, 0:1, :])
    lg_b = _log_sigmoid(dl_ref[1, 0:1, :])
    ri = lax.broadcasted_iota(jnp.int32, (c, c), 0)
    ci = lax.broadcasted_iota(jnp.int32, (c, c), 1)
    rel = (ri - ci).astype(F32)
    dec = (jnp.where(rel >= 0, jnp.exp(jnp.maximum(rel, 0.0) * lg_f[:, :c]), 0.0)
           + jnp.where(rel <= 0, jnp.exp(jnp.maximum(-rel, 0.0) * lg_b[:, :c]), 0.0))
    idx_v = lax.broadcasted_iota(jnp.int32, (c, dv), 0).astype(F32)
    idx_k = lax.broadcasted_iota(jnp.int32, (c, dk), 0).astype(F32)
    xi_f = jnp.exp((idx_v + 1.0) * lg_f)
    xi_b = jnp.exp((c - idx_v) * lg_b)
    zeta_f = jnp.exp((c - 1.0 - idx_k) * lg_f[:, :dk])
    zeta_b = jnp.exp(idx_k * lg_b[:, :dk])
    g_f = jnp.exp(c * lg_f)
    g_b = jnp.exp(c * lg_b)

    def intra(i, carry):
        rows = pl.ds(pl.multiple_of(i * c, c), c)
        q = q_ref[rows, :]
        k = k_ref[rows, :]
        v = jnp.concatenate([v_ref[0, rows, :], v_ref[1, rows, :]], axis=-1)
        qk = lax.dot_general(q, k, (((1,), (1,)), ((), ())), preferred_element_type=F32)
        kf = k.astype(F32)
        lhs = jnp.concatenate([(qk * dec).astype(BF16),
                               (kf * zeta_f).T.astype(BF16),
                               (kf * zeta_b).T.astype(BF16)], axis=0)
        res = jnp.dot(lhs, v, preferred_element_type=F32)
        oi_sc[rows, :] = res[:c]
        uf_sc[i] = res[c:c + dk]
        ub_sc[i] = res[c + dk:]
        return carry

    lax.fori_loop(0, n, intra, 0, unroll=RET_UNROLL)

    def rec_f(i, st):
        st_sc[i, :, 0:dv] = st.astype(BF16)
        return st * g_f + uf_sc[i]

    def rec_b(t, st):
        i = n - 1 - t
        st_sc[i, :, dv:2 * dv] = st.astype(BF16)
        return st * g_b + ub_sc[i]

    lax.fori_loop(0, n, rec_f, jnp.zeros((dk, dv), F32))
    lax.fori_loop(0, n, rec_b, jnp.zeros((dk, dv), F32))

    def cross(i, carry):
        rows = pl.ds(pl.multiple_of(i * c, c), c)
        cr = jnp.dot(q_ref[rows, :], st_sc[i], preferred_element_type=F32)
        o = oi_sc[rows, :] + cr[:, :dv] * xi_f + cr[:, dv:] * xi_b
        y = o * lax.rsqrt(jnp.mean(o * o, axis=-1, keepdims=True) + NORM_EPS)
        z = jnp.concatenate([z_ref[0, rows, :], z_ref[1, rows, :]], axis=-1)
        out = (y * _silu(z)).astype(o_ref.dtype)
        o_ref[0, rows, :] = out[:, :LANE]
        o_ref[1, rows, :] = out[:, LANE:]
        return carry

    lax.fori_loop(0, n, cross, 0, unroll=RET_UNROLL)


def _retention(qkv, z, dl, z_blk2):
    b, _, s, _ = qkv.shape
    nh = N_HEADS_RET
    dv = 2 * LANE
    n = s // RET_CHUNK
    est = (2 * 2 * s * LANE * 2 + 2 * s * dv * 2 + 2 * s * dv * 4 + 2 * s * dv * 2
           + s * dv * 4 + 2 * n * LANE * dv * 4 + n * LANE * 2 * dv * 2)
    return pl.pallas_call(
        functools.partial(_ret_kernel, seq=s),
        grid=(b, nh),
        in_specs=[
            pl.BlockSpec((None, None, s, LANE), lambda bi, h: (bi, h, 0, 0)),
            pl.BlockSpec((None, None, s, LANE), lambda bi, h: (bi, nh + h, 0, 0)),
            pl.BlockSpec((None, 2, s, LANE), lambda bi, h: (bi, nh + h, 0, 0)),
            pl.BlockSpec((None, 2, s, LANE), lambda bi, h: (bi, z_blk2 + h, 0, 0)),
            pl.BlockSpec((None, 2, 8, dv), lambda bi, h: (h, 0, 0, 0)),
        ],
        out_specs=pl.BlockSpec((None, 2, s, LANE), lambda bi, h: (bi, h, 0, 0)),
        out_shape=jax.ShapeDtypeStruct((b, 2 * nh, s, LANE), BF16),
        scratch_shapes=[
            pltpu.VMEM((s, dv), F32),
            pltpu.VMEM((n, LANE, dv), F32),
            pltpu.VMEM((n, LANE, dv), F32),
            pltpu.VMEM((n, LANE, 2 * dv), BF16),
        ],
        compiler_params=pltpu.CompilerParams(
            dimension_semantics=("arbitrary", "arbitrary"),
            vmem_limit_bytes=_vmem_limit(est)),
        name="retention",
    )(qkv, qkv, qkv, z, dl)


def _outproj_kernel(ya_ref, yr_ref, w_ref, x_ref, mod_ref, gn_ref, modn_ref, *rest, last):
    if last:
        o_ref, wbf_ref = rest
    else:
        xo_ref, h_ref, wbf_ref = rest

    @pl.when((pl.program_id(0) == 0) & (pl.program_id(1) == 0))
    def _():
        _cast_weight(w_ref, wbf_ref)

    tm = x_ref.shape[0]
    for s0 in range(0, tm, OUT_SUB_ROWS):
        rows = slice(s0, s0 + OUT_SUB_ROWS)
        y = jnp.concatenate([ya_ref[j, rows, :] for j in range(ya_ref.shape[0])]
                            + [yr_ref[j, rows, :] for j in range(yr_ref.shape[0])], axis=-1)
        out = jnp.dot(y, wbf_ref[...], preferred_element_type=F32)
        xn = x_ref[rows, :] + mod_ref[2:3, :] * out
        if last:
            o_ref[rows, :] = (xn * lax.rsqrt(jnp.mean(xn * xn, axis=-1, keepdims=True) + NORM_EPS)
                              * gn_ref[...])
        else:
            xo_ref[rows, :] = xn
            h = _modulated_norm(xn, gn_ref[...], modn_ref[0:1, :], modn_ref[1:2, :])
            h_ref[rows, :] = h.astype(BF16)


def _outproj(ya, yr, w_out, x, mod, gains, layer, last):
    b, s, d = x.shape
    tm = OUT_ROW_TILE
    nba, nbr = ya.shape[1], yr.shape[1]
    kdim = (nba + nbr) * LANE
    nxt = layer if last else layer + 1
    est = (kdim * d * 4 + kdim * d * 2 + 2 * tm * kdim * 2 + 2 * tm * d * 4
           + 2 * tm * d * 4 + 2 * tm * d * 2 + 3 * OUT_SUB_ROWS * d * 4)
    if last:
        out_shape = jax.ShapeDtypeStruct((b, s, d), F32)
        out_specs = pl.BlockSpec((None, tm, d), lambda bi, m: (bi, m, 0))
    else:
        out_shape = (jax.ShapeDtypeStruct((b, s, d), F32), jax.ShapeDtypeStruct((b, s, d), BF16))
        out_specs = (pl.BlockSpec((None, tm, d), lambda bi, m: (bi, m, 0)),
                     pl.BlockSpec((None, tm, d), lambda bi, m: (bi, m, 0)))
    return pl.pallas_call(
        functools.partial(_outproj_kernel, last=last),
        grid=(b, s // tm),
        in_specs=[
            pl.BlockSpec((None, nba, tm, LANE), lambda bi, m: (bi, 0, m, 0)),
            pl.BlockSpec((None, nbr, tm, LANE), lambda bi, m: (bi, 0, m, 0)),
            pl.BlockSpec((None, kdim, d), lambda bi, m: (layer, 0, 0),
                         pipeline_mode=pl.Buffered(1)),
            pl.BlockSpec((None, tm, d), lambda bi, m: (bi, m, 0)),
            pl.BlockSpec((None, None, 3, d), lambda bi, m: (layer, bi, 0, 0)),
            pl.BlockSpec((None, 1, d), lambda bi, m: (layer + 1, 0, 0)),
            pl.BlockSpec((None, None, 3, d), lambda bi, m: (nxt, bi, 0, 0)),
        ],
        out_specs=out_specs,
        out_shape=out_shape,
        scratch_shapes=[pltpu.VMEM((kdim, d), BF16)],
        compiler_params=pltpu.CompilerParams(
            dimension_semantics=("arbitrary", "arbitrary"),
            vmem_limit_bytes=_vmem_limit(est)),
        name="outproj_last" if last else "outproj",
    )(ya, yr, w_out, x, mod, gains, mod)


def kernel(x, c, norm_gain, w_ada, b_ada, w_in, w_out, ret_decay_logit_f, ret_decay_logit_b,
           final_gain):
    b, s, d = x.shape
    depth = w_in.shape[0]
    attn_w = d // 2
    ret_w = d // 2
    n_heads = attn_w // HEAD_DIM_ATTN
    ret_v = ret_w // N_HEADS_RET
    ret_qk = ret_v // 2
    assert ret_qk == LANE and ret_v == 2 * LANE and HEAD_DIM_ATTN == LANE
    assert attn_w == IN_COL_TILE and N_HEADS_RET * ret_qk * 2 == IN_COL_TILE
    assert w_in.shape[2] == 7 * IN_COL_TILE and s % ROW_TILE == 0
    dilations = tuple(dl for _, dl in DILATION_PATTERNS)
    assert all(IN_SUB_ROWS % (dl * 16) == 0 for dl in dilations) and ROW_TILE % IN_SUB_ROWS == 0
    assert dilations[0] == 1 and all(b_ % a_ == 0 for a_, b_ in zip(dilations, dilations[1:]))

    colscale = jnp.ones((7, IN_COL_TILE), F32)
    colscale = colscale.at[0, :].set(HEAD_DIM_ATTN ** -0.5 * LOG2_E)
    colscale = colscale.at[4, N_HEADS_RET * ret_qk:].set(ret_qk ** -0.5)
    colscale = colscale.reshape(1, 7 * IN_COL_TILE)

    c_pad = jnp.zeros((C_PAD, d), F32).at[:b].set(c)
    mod = _adaln(c_pad, w_ada, b_ada).reshape(depth, C_PAD, 3, d)
    gains = jnp.concatenate([norm_gain, final_gain[None]], axis=0).reshape(depth + 1, 1, d)

    def lanes(v):
        return jnp.broadcast_to(v[:, :, None, None, None], (depth, N_HEADS_RET, 1, 8, 2 * LANE))
    dl = jnp.concatenate([lanes(ret_decay_logit_f), lanes(ret_decay_logit_b)], axis=2)

    h = _modulate(x, gains, mod, 0)
    out = None
    for layer in range(depth):
        qkv_a = _inproj_attn(h, w_in, colscale, layer, 3, lambda j: j, dilations)
        qkv_r = _inproj(h, w_in, colscale, layer, 2, lambda j: j + 4, BF16, "inproj_ret")
        z = _inproj(h, w_in, colscale, layer, 2, lambda j: 3 + 3 * j, F32, "inproj_gate")
        ya = _attention(qkv_a, z, n_heads)
        yr = _retention(qkv_r, z, dl[layer], n_heads // 2)
        last = layer == depth - 1
        res = _outproj(ya, yr, w_out, x, mod, gains, layer, last)
        if last:
            out = res
        else:
            x, h = res
    return out
```

```python
import functools

import jax
import jax.numpy as jnp
from jax import lax
from jax.experimental import pallas as pl
from jax.experimental.pallas import tpu as pltpu

F32 = jnp.float32
BF16 = jnp.bfloat16

HEAD_DIM_ATTN = 128
DILATION_PATTERNS = ((128, 1), (512, 4), (2048, 16))
N_HEADS_RET = 4
RET_CHUNK = 128
NORM_EPS = 1e-6
LOG2_E = 1.4426950408889634

LANE = 128
VMEM_CAP_BYTES = 60000 * 1024
TEMP_ALLOWANCE = 8 * 2**20

ROW_TILE = 1024
NORM_ROW_TILE = 512
OUT_ROW_TILE = 512
IN_COL_TILE = 1024
ADA_COL_TILE = 1024
ATTN_TQ = 128
ATTN_UNROLL = 32
RET_UNROLL = 32
IN_SUB_ROWS = 256
OUT_SUB_ROWS = 128
CAST_ROWS = 256
C_PAD = 16


def _vmem_limit(nbytes):
    return int(min(VMEM_CAP_BYTES, nbytes + TEMP_ALLOWANCE))


def _silu(v):
    return v * jax.nn.sigmoid(v)


def _cast_weight(w_ref, wbf_ref):
    n = w_ref.shape[0] // CAST_ROWS

    def body(i, carry):
        rows = pl.ds(pl.multiple_of(i * CAST_ROWS, CAST_ROWS), CAST_ROWS)
        wbf_ref[rows, :] = w_ref[rows, :].astype(BF16)
        return carry

    lax.fori_loop(0, n, body, 0)


def _ada_kernel(c_ref, w_ref, b_ref, o_ref):
    c_act = _silu(c_ref[...])
    acc = jnp.dot(c_act.astype(BF16), w_ref[...].astype(BF16), preferred_element_type=F32)
    o_ref[...] = acc + b_ref[...]


def _adaln(c_pad, w_ada, b_ada):
    depth, d, n3 = w_ada.shape
    tn = ADA_COL_TILE
    est = 2 * d * tn * 4 + d * tn * 2 + 4 * C_PAD * (d + 2 * tn) * 4
    return pl.pallas_call(
        _ada_kernel,
        grid=(depth, n3 // tn),
        in_specs=[
            pl.BlockSpec((C_PAD, d), lambda l, j: (0, 0)),
            pl.BlockSpec((None, d, tn), lambda l, j: (l, 0, j)),
            pl.BlockSpec((None, 1, tn), lambda l, j: (l, 0, j)),
        ],
        out_specs=pl.BlockSpec((None, C_PAD, tn), lambda l, j: (l, 0, j)),
        out_shape=jax.ShapeDtypeStruct((depth, C_PAD, n3), F32),
        compiler_params=pltpu.CompilerParams(
            dimension_semantics=("arbitrary", "arbitrary"),
            vmem_limit_bytes=_vmem_limit(est)),
        name="adaln_mod",
    )(c_pad, w_ada, b_ada.reshape(depth, 1, n3))


def _modulated_norm(x, g, shift, scale):
    y = x * lax.rsqrt(jnp.mean(x * x, axis=-1, keepdims=True) + NORM_EPS)
    return y * (g * (1.0 + scale)) + shift


def _modulate_kernel(x_ref, g_ref, mod_ref, h_ref):
    h = _modulated_norm(x_ref[...], g_ref[...], mod_ref[0:1, :], mod_ref[1:2, :])
    h_ref[...] = h.astype(BF16)


def _modulate(x, gains, mod, layer):
    b, s, d = x.shape
    tm = NORM_ROW_TILE
    est = 2 * tm * d * (4 + 2) + 3 * tm * d * 4
    return pl.pallas_call(
        _modulate_kernel,
        grid=(b, s // tm),
        in_specs=[
            pl.BlockSpec((None, tm, d), lambda bi, m: (bi, m, 0)),
            pl.BlockSpec((None, 1, d), lambda bi, m: (layer, 0, 0)),
            pl.BlockSpec((None, None, 3, d), lambda bi, m: (layer, bi, 0, 0)),
        ],
        out_specs=pl.BlockSpec((None, tm, d), lambda bi, m: (bi, m, 0)),
        out_shape=jax.ShapeDtypeStruct((b, s, d), BF16),
        compiler_params=pltpu.CompilerParams(
            dimension_semantics=("arbitrary", "arbitrary"),
            vmem_limit_bytes=_vmem_limit(est)),
        name="modulate0",
    )(x, gains, mod)


def _project_tile(h_ref, w_ref, cs_ref, wbf_ref):
    @pl.when((pl.program_id(1) == 0) & (pl.program_id(2) == 0))
    def _():
        _cast_weight(w_ref, wbf_ref)

    acc = jnp.dot(h_ref[...], wbf_ref[...], preferred_element_type=F32)
    return acc * cs_ref[...]


def _inproj_kernel(h_ref, w_ref, cs_ref, o_ref, wbf_ref):
    acc = _project_tile(h_ref, w_ref, cs_ref, wbf_ref)
    for j in range(o_ref.shape[0]):
        o_ref[j] = acc[:, j * LANE:(j + 1) * LANE].astype(o_ref.dtype)


def _inproj_attn_kernel(h_ref, w_ref, cs_ref, *rest, dilations):
    nd = len(dilations)
    out_refs = rest[:nd]
    wbf_ref = rest[nd]
    stage_refs = rest[nd + 1:]

    @pl.when((pl.program_id(1) == 0) & (pl.program_id(2) == 0))
    def _():
        _cast_weight(w_ref, wbf_ref)

    nb, sub = stage_refs[0].shape[0], stage_refs[0].shape[2]
    for t in range(len(stage_refs) // (nd - 1)):
        stages = stage_refs[t * (nd - 1):(t + 1) * (nd - 1)]
        s0 = t * sub
        acc = jnp.dot(h_ref[s0:s0 + sub, :], wbf_ref[...], preferred_element_type=F32)
        acc = acc * cs_ref[...]
        for j in range(nb):
            blk = acc[:, j * LANE:(j + 1) * LANE]
            stages[0][j, 0] = blk
            out_refs[0][j, 0, s0:s0 + sub, :] = blk.astype(out_refs[0].dtype)
        for k in range(1, nd):
            prev, dil = dilations[k - 1], dilations[k]
            q, n = dil // prev, sub // dil
            for j in range(nb):
                for r in range(dil):
                    ra, rb = r % prev, r // prev
                    v = stages[k - 1][j, ra, pl.ds(rb, n, stride=q), :]
                    out_refs[k][j, r, s0 // dil:s0 // dil + n, :] = v.astype(out_refs[k].dtype)
                    if k < nd - 1:
                        stages[k][j, r] = v


def _inproj_specs(h, layer, tile_of):
    b, s, d = h.shape
    tm, tn = ROW_TILE, IN_COL_TILE
    return [
        pl.BlockSpec((None, tm, d), lambda j, bi, m: (bi, m, 0)),
        pl.BlockSpec((None, d, tn), lambda j, bi, m: (layer, 0, tile_of(j))),
        pl.BlockSpec((1, tn), lambda j, bi, m: (0, tile_of(j))),
    ]


def _inproj(h, w_in, colscale, layer, n_tiles, tile_of, out_dtype, name):
    b, s, d = h.shape
    tm, tn = ROW_TILE, IN_COL_TILE
    nb = tn // LANE
    osz = jnp.dtype(out_dtype).itemsize
    est = 2 * d * tn * 4 + d * tn * 2 + 2 * tm * d * 2 + 2 * tm * tn * osz + 2 * tm * tn * 4
    return pl.pallas_call(
        _inproj_kernel,
        grid=(n_tiles, b, s // tm),
        in_specs=_inproj_specs(h, layer, tile_of),
        out_specs=pl.BlockSpec((None, nb, tm, LANE), lambda j, bi, m: (bi, j, m, 0)),
        out_shape=jax.ShapeDtypeStruct((b, n_tiles * nb, s, LANE), out_dtype),
        scratch_shapes=[pltpu.VMEM((d, tn), BF16)],
        compiler_params=pltpu.CompilerParams(
            dimension_semantics=("arbitrary", "arbitrary", "arbitrary"),
            vmem_limit_bytes=_vmem_limit(est)),
        name=name,
    )(h, w_in, colscale)


def _inproj_attn(h, w_in, colscale, layer, n_tiles, tile_of, dilations):
    b, s, d = h.shape
    tm, tn = ROW_TILE, IN_COL_TILE
    nb = tn // LANE
    nd = len(dilations)
    est = (2 * d * tn * 4 + d * tn * 2 + 2 * tm * d * 2 + nd * 2 * tm * tn * 2
           + 3 * tm * tn * 4)
    return pl.pallas_call(
        functools.partial(_inproj_attn_kernel, dilations=dilations),
        grid=(n_tiles, b, s // tm),
        in_specs=_inproj_specs(h, layer, tile_of),
        out_specs=[pl.BlockSpec((None, nb, dl, tm // dl, LANE),
                                lambda j, bi, m: (bi, j, 0, m, 0)) for dl in dilations],
        out_shape=[jax.ShapeDtypeStruct((b, n_tiles * nb, dl, s // dl, LANE), BF16)
                   for dl in dilations],
        scratch_shapes=[pltpu.VMEM((d, tn), BF16)]
        + [pltpu.VMEM((nb, dl, IN_SUB_ROWS // dl, LANE), F32)
           for dl in dilations[:-1]] * (tm // IN_SUB_ROWS),
        compiler_params=pltpu.CompilerParams(
            dimension_semantics=("arbitrary", "arbitrary", "arbitrary"),
            vmem_limit_bytes=_vmem_limit(est)),
        name="inproj_attn",
    )(h, w_in, colscale)


def _park_pitch(dil):
    return dil + 4 if dil % 16 == 0 else dil


def _attn_kernel(*refs, n_heads, patterns, seq):
    npat = len(patterns)
    qkv_refs = refs[:3 * npat]
    z_ref, o_ref = refs[3 * npat:3 * npat + 2]
    bias_sc = refs[-1]
    park = refs[3 * npat + 2:-1]
    tq = ATTN_TQ
    radius = patterns[0][0] // (2 * patterns[0][1])
    tk = tq + 2 * radius

    head = pl.program_id(1)
    expo = (head + 1).astype(F32) * (-8.0 / n_heads)
    slope = jnp.exp2(jnp.full((tq, tk), 1.0, F32) * expo)
    row = lax.broadcasted_iota(jnp.int32, (tq, tk), 0)
    col = lax.broadcasted_iota(jnp.int32, (tq, tk), 1)
    for p, (_, dil) in enumerate(patterns):
        for case in range(3):
            rel = jnp.abs(col - row - case * radius)
            bias = -slope * (rel * dil).astype(F32) * LOG2_E
            bias_sc[p * 3 + case] = jnp.where(rel <= radius, bias, -jnp.inf)

    nblocks = seq // tq
    strided = [p for p in range(npat) if patterns[p][1] != 1]
    slot_of = {p: i for i, p in enumerate(strided)}
    for p in strided + [p for p in range(npat) if patterns[p][1] == 1]:
        dil = patterns[p][1]
        sub_len = seq // dil
        nblk = sub_len // tq
        shift = nblk.bit_length() - 1
        q_ref, k_ref, v_ref = qkv_refs[3 * p:3 * p + 3]

        def block(g, carry, p=p, dil=dil, sub_len=sub_len, nblk=nblk, shift=shift,
                  q_ref=q_ref, k_ref=k_ref, v_ref=v_ref):
            r = lax.shift_right_logical(g, shift)
            i = lax.bitwise_and(g, nblk - 1)
            l0 = pl.multiple_of(i * tq, tq)
            ks = pl.multiple_of(jnp.clip(l0 - radius, 0, sub_len - tk), radius)
            case = (l0 - ks) // radius
            q = q_ref[r, pl.ds(l0, tq), :]
            k = k_ref[r, pl.ds(ks, tk), :]
            v = v_ref[r, pl.ds(ks, tk), :]
            s = lax.dot_general(q, k, (((1,), (1,)), ((), ())), preferred_element_type=F32)
            s = s + bias_sc[p * 3 + case]
            m = jnp.max(s, axis=-1, keepdims=True)
            pe = jnp.exp2(s - m)
            v1 = jnp.concatenate([v, jnp.ones((tk, LANE), BF16)], axis=-1)
            av = jnp.dot(pe.astype(BF16), v1, preferred_element_type=F32)
            acc, den = av[:, :LANE], av[:, LANE:]
            if dil != 1:
                pitch = _park_pitch(dil)
                rows = pl.ds(l0 * pitch + r, tq, stride=pitch)
                acc_ref, m_ref, den_ref = park[3 * slot_of[p]:3 * slot_of[p] + 3]
                acc_ref[rows, :] = acc
                m_ref[rows, :] = jnp.broadcast_to(m, (tq, LANE))
                den_ref[rows, :] = jnp.broadcast_to(den, (tq, LANE))
                return carry

            def parked(ref, dl):
                pitch = _park_pitch(dl)
                if pitch == dl:
                    return ref[pl.ds(l0, tq), :]
                first = l0 // dl
                return jnp.concatenate(
                    [ref[pl.ds(pl.multiple_of((first + g) * pitch, 4), dl), :]
                     for g in range(tq // dl)], axis=0)

            dils = [patterns[i][1] for i in strided]
            others = [parked(park[3 * i + 1], dl) for i, dl in enumerate(dils)]
            mx = jnp.maximum(functools.reduce(jnp.maximum, others), m)
            e = jnp.exp2(m - mx)
            num = e * acc
            wsum = e * den
            for i, (ms, dl) in enumerate(zip(others, dils)):
                e = jnp.exp2(ms - mx)
                num = num + e * parked(park[3 * i], dl)
                wsum = wsum + e * parked(park[3 * i + 2], dl)
            rows = pl.ds(l0, tq)
            o_ref[rows, :] = ((num / wsum) * _silu(z_ref[rows, :])).astype(o_ref.dtype)
            return carry

        lax.fori_loop(0, nblocks, block, 0, unroll=ATTN_UNROLL)


def _attention(qkv_views, z, n_heads):
    patterns = DILATION_PATTERNS
    b, _, _, s, _ = qkv_views[0].shape
    radius = patterns[0][0] // (2 * patterns[0][1])
    assert all(w // (2 * dl) == radius for w, dl in patterns)
    tk = ATTN_TQ + 2 * radius
    for _, dl in patterns:
        nblk = (s // dl) // ATTN_TQ
        assert (s // dl) % ATTN_TQ == 0 and s // dl >= tk and nblk & (nblk - 1) == 0
    assert (s // ATTN_TQ) % ATTN_UNROLL == 0
    npat = len(patterns)

    operands, in_specs = [], []
    for (_, dl), view in zip(patterns, qkv_views):
        for t in range(3):
            operands.append(view)
            in_specs.append(pl.BlockSpec(
                (None, None, dl, s // dl, LANE),
                lambda bi, h, t=t: (bi, t * n_heads + h, 0, 0, 0)))
    operands.append(z)
    in_specs.append(pl.BlockSpec((None, None, s, LANE), lambda bi, h: (bi, h, 0, 0)))

    assert sum(dl == 1 for _, dl in patterns) == 1 and npat > 1
    park_rows = [(s // dl) * _park_pitch(dl) for _, dl in patterns if dl != 1]
    est = (2 * 3 * npat * s * LANE * 2 + 2 * s * LANE * 4 + 2 * s * LANE * 2
           + 3 * sum(park_rows) * LANE * 4 + 3 * npat * ATTN_TQ * tk * 4)
    return pl.pallas_call(
        functools.partial(_attn_kernel, n_heads=n_heads, patterns=patterns, seq=s),
        grid=(b, n_heads),
        in_specs=in_specs,
        out_specs=pl.BlockSpec((None, None, s, LANE), lambda bi, h: (bi, h, 0, 0)),
        out_shape=jax.ShapeDtypeStruct((b, n_heads, s, LANE), BF16),
        scratch_shapes=[pltpu.VMEM((rows, LANE), F32) for rows in park_rows for _ in range(3)]
        + [pltpu.VMEM((3 * npat, ATTN_TQ, tk), F32)],
        compiler_params=pltpu.CompilerParams(
            dimension_semantics=("arbitrary", "arbitrary"),
            vmem_limit_bytes=_vmem_limit(est)),
        name="dilated_attention",
    )(*operands)


def _log_sigmoid(v):
    return jnp.minimum(v, 0.0) - jnp.log1p(jnp.exp(-jnp.abs(v)))


def _ret_kernel(q_ref, k_ref, v_ref, z_ref, dl_ref, o_ref, oi_sc, uf_sc, ub_sc, st_sc, *, seq):
    c = RET_CHUNK
    n = seq // c
    dk = q_ref.shape[-1]
    dv = 2 * LANE

    lg_f = _log_sigmoid(dl_ref[0, 0:1, :])
    lg_b = _log_sigmoid(dl_ref[1, 0:1, :])
    ri = lax.broadcasted_iota(jnp.int32, (c, c), 0)
    ci = lax.broadcasted_iota(jnp.int32, (c, c), 1)
    rel = (ri - ci).astype(F32)
    dec = (jnp.where(rel >= 0, jnp.exp(jnp.maximum(rel, 0.0) * lg_f[:, :c]), 0.0)
           + jnp.where(rel <= 0, jnp.exp(jnp.maximum(-rel, 0.0) * lg_b[:, :c]), 0.0))
    idx_v = lax.broadcasted_iota(jnp.int32, (c, dv), 0).astype(F32)
    idx_k = lax.broadcasted_iota(jnp.int32, (c, dk), 0).astype(F32)
    xi_f = jnp.exp((idx_v + 1.0) * lg_f)
    xi_b = jnp.exp((c - idx_v) * lg_b)
    zeta_f = jnp.exp((c - 1.0 - idx_k) * lg_f[:, :dk])
    zeta_b = jnp.exp(idx_k * lg_b[:, :dk])
    g_f = jnp.exp(c * lg_f)
    g_b = jnp.exp(c * lg_b)

    def intra(i, carry):
        rows = pl.ds(pl.multiple_of(i * c, c), c)
        q = q_ref[rows, :]
        k = k_ref[rows, :]
        v = jnp.concatenate([v_ref[0, rows, :], v_ref[1, rows, :]], axis=-1)
        qk = lax.dot_general(q, k, (((1,), (1,)), ((), ())), preferred_element_type=F32)
        kf = k.astype(F32)
        lhs = jnp.concatenate([(qk * dec).astype(BF16),
                               (kf * zeta_f).T.astype(BF16),
                               (kf * zeta_b).T.astype(BF16)], axis=0)
        res = jnp.dot(lhs, v, preferred_element_type=F32)
        oi_sc[rows, :] = res[:c]
        uf_sc[i] = res[c:c + dk]
        ub_sc[i] = res[c + dk:]
        return carry

    lax.fori_loop(0, n, intra, 0, unroll=RET_UNROLL)

    def rec_f(i, st):
        st_sc[i, :, 0:dv] = st.astype(BF16)
        return st * g_f + uf_sc[i]

    def rec_b(t, st):
        i = n - 1 - t
        st_sc[i, :, dv:2 * dv] = st.astype(BF16)
        return st * g_b + ub_sc[i]

    lax.fori_loop(0, n, rec_f, jnp.zeros((dk, dv), F32))
    lax.fori_loop(0, n, rec_b, jnp.zeros((dk, dv), F32))

    def cross(i, carry):
        rows = pl.ds(pl.multiple_of(i * c, c), c)
        cr = jnp.dot(q_ref[rows, :], st_sc[i], preferred_element_type=F32)
        o = oi_sc[rows, :] + cr[:, :dv] * xi_f + cr[:, dv:] * xi_b
        y = o * lax.rsqrt(jnp.mean(o * o, axis=-1, keepdims=True) + NORM_EPS)
        z = jnp.concatenate([z_ref[0, rows, :], z_ref[1, rows, :]], axis=-1)
        out = (y * _silu(z)).astype(o_ref.dtype)
        o_ref[0, rows, :] = out[:, :LANE]
        o_ref[1, rows, :] = out[:, LANE:]
        return carry

    lax.fori_loop(0, n, cross, 0, unroll=RET_UNROLL)


def _retention(qkv, z, dl, z_blk2):
    b, _, s, _ = qkv.shape
    nh = N_HEADS_RET
    dv = 2 * LANE
    n = s // RET_CHUNK
    est = (2 * 2 * s * LANE * 2 + 2 * s * dv * 2 + 2 * s * dv * 4 + 2 * s * dv * 2
           + s * dv * 4 + 2 * n * LANE * dv * 4 + n * LANE * 2 * dv * 2)
    return pl.pallas_call(
        functools.partial(_ret_kernel, seq=s),
        grid=(b, nh),
        in_specs=[
            pl.BlockSpec((None, None, s, LANE), lambda bi, h: (bi, h, 0, 0)),
            pl.BlockSpec((None, None, s, LANE), lambda bi, h: (bi, nh + h, 0, 0)),
            pl.BlockSpec((None, 2, s, LANE), lambda bi, h: (bi, nh + h, 0, 0)),
            pl.BlockSpec((None, 2, s, LANE), lambda bi, h: (bi, z_blk2 + h, 0, 0)),
            pl.BlockSpec((None, 2, 8, dv), lambda bi, h: (h, 0, 0, 0)),
        ],
        out_specs=pl.BlockSpec((None, 2, s, LANE), lambda bi, h: (bi, h, 0, 0)),
        out_shape=jax.ShapeDtypeStruct((b, 2 * nh, s, LANE), BF16),
        scratch_shapes=[
            pltpu.VMEM((s, dv), F32),
            pltpu.VMEM((n, LANE, dv), F32),
            pltpu.VMEM((n, LANE, dv), F32),
            pltpu.VMEM((n, LANE, 2 * dv), BF16),
        ],
        compiler_params=pltpu.CompilerParams(
            dimension_semantics=("arbitrary", "arbitrary"),
            vmem_limit_bytes=_vmem_limit(est)),
        name="retention",
    )(qkv, qkv, qkv, z, dl)


def _outproj_kernel(ya_ref, yr_ref, w_ref, x_ref, mod_ref, gn_ref, modn_ref, *rest, last):
    if last:
        o_ref, wbf_ref = rest
    else:
        xo_ref, h_ref, wbf_ref = rest

    @pl.when((pl.program_id(0) == 0) & (pl.program_id(1) == 0))
    def _():
        _cast_weight(w_ref, wbf_ref)

    tm = x_ref.shape[0]
    for s0 in range(0, tm, OUT_SUB_ROWS):
        rows = slice(s0, s0 + OUT_SUB_ROWS)
        y = jnp.concatenate([ya_ref[j, rows, :] for j in range(ya_ref.shape[0])]
                            + [yr_ref[j, rows, :] for j in range(yr_ref.shape[0])], axis=-1)
        out = jnp.dot(y, wbf_ref[...], preferred_element_type=F32)
        xn = x_ref[rows, :] + mod_ref[2:3, :] * out
        if last:
            o_ref[rows, :] = (xn * lax.rsqrt(jnp.mean(xn * xn, axis=-1, keepdims=True) + NORM_EPS)
                              * gn_ref[...])
        else:
            xo_ref[rows, :] = xn
            h = _modulated_norm(xn, gn_ref[...], modn_ref[0:1, :], modn_ref[1:2, :])
            h_ref[rows, :] = h.astype(BF16)


def _outproj(ya, yr, w_out, x, mod, gains, layer, last):
    b, s, d = x.shape
    tm = OUT_ROW_TILE
    nba, nbr = ya.shape[1], yr.shape[1]
    kdim = (nba + nbr) * LANE
    nxt = layer if last else layer + 1
    est = (kdim * d * 4 + kdim * d * 2 + 2 * tm * kdim * 2 + 2 * tm * d * 4
           + 2 * tm * d * 4 + 2 * tm * d * 2 + 3 * OUT_SUB_ROWS * d * 4)
    if last:
        out_shape = jax.ShapeDtypeStruct((b, s, d), F32)
        out_specs = pl.BlockSpec((None, tm, d), lambda bi, m: (bi, m, 0))
    else:
        out_shape = (jax.ShapeDtypeStruct((b, s, d), F32), jax.ShapeDtypeStruct((b, s, d), BF16))
        out_specs = (pl.BlockSpec((None, tm, d), lambda bi, m: (bi, m, 0)),
                     pl.BlockSpec((None, tm, d), lambda bi, m: (bi, m, 0)))
    return pl.pallas_call(
        functools.partial(_outproj_kernel, last=last),
        grid=(b, s // tm),
        in_specs=[
            pl.BlockSpec((None, nba, tm, LANE), lambda bi, m: (bi, 0, m, 0)),
            pl.BlockSpec((None, nbr, tm, LANE), lambda bi, m: (bi, 0, m, 0)),
            pl.BlockSpec((None, kdim, d), lambda bi, m: (layer, 0, 0),
                         pipeline_mode=pl.Buffered(1)),
            pl.BlockSpec((None, tm, d), lambda bi, m: (bi, m, 0)),
            pl.BlockSpec((None, None, 3, d), lambda bi, m: (layer, bi, 0, 0)),
            pl.BlockSpec((None, 1, d), lambda bi, m: (layer + 1, 0, 0)),
            pl.BlockSpec((None, None, 3, d), lambda bi, m: (nxt, bi, 0, 0)),
        ],
        out_specs=out_specs,
        out_shape=out_shape,
        scratch_shapes=[pltpu.VMEM((kdim, d), BF16)],
        compiler_params=pltpu.CompilerParams(
            dimension_semantics=("arbitrary", "arbitrary"),
            vmem_limit_bytes=_vmem_limit(est)),
        name="outproj_last" if last else "outproj",
    )(ya, yr, w_out, x, mod, gains, mod)


def kernel(x, c, norm_gain, w_ada, b_ada, w_in, w_out, ret_decay_logit_f, ret_decay_logit_b,
           final_gain):
    b, s, d = x.shape
    depth = w_in.shape[0]
    attn_w = d // 2
    ret_w = d // 2
    n_heads = attn_w // HEAD_DIM_ATTN
    ret_v = ret_w // N_HEADS_RET
    ret_qk = ret_v // 2
    assert ret_qk == LANE and ret_v == 2 * LANE and HEAD_DIM_ATTN == LANE
    assert attn_w == IN_COL_TILE and N_HEADS_RET * ret_qk * 2 == IN_COL_TILE
    assert w_in.shape[2] == 7 * IN_COL_TILE and s % ROW_TILE == 0
    dilations = tuple(dl for _, dl in DILATION_PATTERNS)
    assert all(IN_SUB_ROWS % (dl * 16) == 0 for dl in dilations) and ROW_TILE % IN_SUB_ROWS == 0
    assert dilations[0] == 1 and all(b_ % a_ == 0 for a_, b_ in zip(dilations, dilations[1:]))

    colscale = jnp.ones((7, IN_COL_TILE), F32)
    colscale = colscale.at[0, :].set(HEAD_DIM_ATTN ** -0.5 * LOG2_E)
    colscale = colscale.at[4, N_HEADS_RET * ret_qk:].set(ret_qk ** -0.5)
    colscale = colscale.reshape(1, 7 * IN_COL_TILE)

    c_pad = jnp.zeros((C_PAD, d), F32).at[:b].set(c)
    mod = _adaln(c_pad, w_ada, b_ada).reshape(depth, C_PAD, 3, d)
    gains = jnp.concatenate([norm_gain, final_gain[None]], axis=0).reshape(depth + 1, 1, d)

    def lanes(v):
        return jnp.broadcast_to(v[:, :, None, None, None], (depth, N_HEADS_RET, 1, 8, 2 * LANE))
    dl = jnp.concatenate([lanes(ret_decay_logit_f), lanes(ret_decay_logit_b)], axis=2)

    h = _modulate(x, gains, mod, 0)
    out = None
    for layer in range(depth):
        qkv_a = _inproj_attn(h, w_in, colscale, layer, 3, lambda j: j, dilations)
        qkv_r = _inproj(h, w_in, colscale, layer, 2, lambda j: j + 4, BF16, "inproj_ret")
        z = _inproj(h, w_in, colscale, layer, 2, lambda j: 3 + 3 * j, F32, "inproj_gate")
        ya = _attention(qkv_a, z, n_heads)
        yr = _retention(qkv_r, z, dl[layer], n_heads // 2)
        last = layer == depth - 1
        res = _outproj(ya, yr, w_out, x, mod, gains, layer, last)
        if last:
            out = res
        else:
            x, h = res
    return out
```
